```python
import jax, jax.numpy as jnp
from jax import lax
import numpy as np

D_MODEL = 2048
BATCH = 4
SEQ = 4096
DEPTH = 2

N_A_LAYERS = DEPTH // 2
N_B_LAYERS = DEPTH - N_A_LAYERS
N_DENSE = (DEPTH + 1) // 2
N_MOE = DEPTH // 2

POOL_WINDOWS = (2, 4, 8, 16)
N_POOL_GROUPS = 4
POOL_GROUP_DIM = D_MODEL // N_POOL_GROUPS

HEAD_DIM = 128
N_HEADS = D_MODEL // HEAD_DIM
MOBA_BLOCK = 256
MOBA_TOPK = 3
MOBA_QCHUNK = 16
ROPE_THETA = 10000.0

D_FF = ((8 * D_MODEL // 3 + 255) // 256) * 256
N_EXPERTS = 8
MOE_TOPK = 2
D_FF_EXPERT = 7 * D_MODEL // 2
MOE_BLOCK = 512

DEEPNORM_ALPHA = (2.0 * DEPTH) ** 0.25
DEEPNORM_BETA = (8.0 * DEPTH) ** -0.25
LN_EPS = 1e-5
NEG = -1e30

kernel_name = 'yoco_pool_moba_moe_deepnorm'


def layer_norm(x, g, b):
    xf = x.astype(jnp.float32)
    mu = jnp.mean(xf, axis=-1, keepdims=True)
    var = jnp.mean(jnp.square(xf - mu), axis=-1, keepdims=True)
    y = (xf - mu) * lax.rsqrt(var + LN_EPS)
    return (y * g.astype(jnp.float32) + b.astype(jnp.float32)).astype(x.dtype)


def pool_mixer(x, w_groups, scale):
    B, S, D = x.shape
    xg = x.reshape(B, S, N_POOL_GROUPS, POOL_GROUP_DIM)
    csum = jnp.cumsum(xg.astype(jnp.float32), axis=1)
    t = jnp.arange(S)
    pooled = []
    for g, w in enumerate(POOL_WINDOWS):
        c = csum[:, :, g]
        lag = jnp.pad(c[:, :S - w], ((0, 0), (w, 0), (0, 0)))
        cnt = jnp.minimum(t + 1, w).astype(jnp.float32)[None, :, None]
        pooled.append((c - lag) / cnt)
    pooled = jnp.stack(pooled, axis=2)
    diff = (pooled - xg.astype(jnp.float32)).astype(x.dtype)
    y = jnp.einsum('bsgc,gcd->bsgd', diff, w_groups)
    return y.reshape(B, S, D) * scale


def rope_tables(S):
    inv = 1.0 / (ROPE_THETA ** (jnp.arange(0, HEAD_DIM, 2, dtype=jnp.float32) / HEAD_DIM))
    ang = jnp.arange(S, dtype=jnp.float32)[:, None] * inv[None, :]
    ang = jnp.concatenate([ang, ang], axis=-1)
    return jnp.cos(ang), jnp.sin(ang)


def apply_rope(t, cos, sin):
    tf = t.astype(jnp.float32)
    t1, t2 = jnp.split(tf, 2, axis=-1)
    rot = jnp.concatenate([-t2, t1], axis=-1)
    return (tf * cos + rot * sin).astype(t.dtype)


def split_heads(t):
    B, S, _ = t.shape
    return t.reshape(B, S, N_HEADS, HEAD_DIM).transpose(0, 2, 1, 3)


def shared_kv(h, w_kv, cos, sin):
    B, S, D = h.shape
    k, v = jnp.split(h @ w_kv, 2, axis=-1)
    k = apply_rope(split_heads(k), cos, sin)
    v = split_heads(v)
    nb = -(-S // MOBA_BLOCK)
    pad = nb * MOBA_BLOCK - S
    k = jnp.pad(k, ((0, 0), (0, 0), (0, pad), (0, 0)))
    v = jnp.pad(v, ((0, 0), (0, 0), (0, pad), (0, 0)))
    k_blocks = k.reshape(B, N_HEADS, nb, MOBA_BLOCK, HEAD_DIM)
    v_blocks = v.reshape(B, N_HEADS, nb, MOBA_BLOCK, HEAD_DIM)
    k_mean = jnp.mean(k_blocks.astype(jnp.float32), axis=3).astype(k.dtype)
    return k_blocks, v_blocks, k_mean


def moba_attention(x, w_q, w_o, k_blocks, v_blocks, k_mean, cos, sin):
    B, S, D = x.shape
    nb = k_blocks.shape[2]
    topk = min(MOBA_TOPK, nb)
    q = apply_rope(split_heads(x @ w_q), cos, sin)
    gate = jnp.einsum('bhsd,bhnd->bhsn', q, k_mean).astype(jnp.float32)
    q_blk = jnp.arange(S) // MOBA_BLOCK
    past = jnp.arange(nb)[None, :] < q_blk[:, None]
    gate = jnp.where(past, gate, NEG)
    _, sel = lax.top_k(gate, topk)
    scale = HEAD_DIM ** -0.5
    gather = jax.vmap(jax.vmap(lambda blocks, ix: blocks[ix]))
    sel_len = topk * MOBA_BLOCK

    def chunk(i):
        c0 = i * MOBA_QCHUNK
        qc = lax.dynamic_slice_in_dim(q, c0, MOBA_QCHUNK, axis=2)
        ic = lax.dynamic_slice_in_dim(sel, c0, MOBA_QCHUNK, axis=2)
        own = c0 // MOBA_BLOCK
        k_sel = gather(k_blocks, ic)
        v_sel = gather(v_blocks, ic)
        k_own = lax.dynamic_index_in_dim(k_blocks, own, axis=2, keepdims=False)
        v_own = lax.dynamic_index_in_dim(v_blocks, own, axis=2, keepdims=False)
        s_sel = jnp.einsum('bhqd,bhqjkd->bhqjk', qc, k_sel).astype(jnp.float32) * scale
        s_own = jnp.einsum('bhqd,bhkd->bhqk', qc, k_own).astype(jnp.float32) * scale
        qpos = c0 + jnp.arange(MOBA_QCHUNK)
        kpos = own * MOBA_BLOCK + jnp.arange(MOBA_BLOCK)
        s_own = jnp.where(kpos[None, :] <= qpos[:, None], s_own, NEG)
        slot_ok = jnp.arange(topk) < own
        s_sel = jnp.where(slot_ok[:, None], s_sel, NEG)
        s = jnp.concatenate([s_sel.reshape(B, N_HEADS, MOBA_QCHUNK, sel_len), s_own], axis=-1)
        p = jax.nn.softmax(s, axis=-1).astype(x.dtype)
        p_sel = p[..., :sel_len].reshape(B, N_HEADS, MOBA_QCHUNK, topk, MOBA_BLOCK)
        p_own = p[..., sel_len:]
        return (jnp.einsum('bhqjk,bhqjkd->bhqd', p_sel, v_sel)
                + jnp.einsum('bhqk,bhkd->bhqd', p_own, v_own))

    o = lax.map(chunk, jnp.arange(S // MOBA_QCHUNK))
    o = o.transpose(1, 0, 3, 2, 4).reshape(B, S, D)
    return o @ w_o


def swiglu(x, w_gate, w_up, w_down):
    return (jax.nn.silu(x @ w_gate) * (x @ w_up)) @ w_down


def moe_ffn(x, w_router, w_gate, w_up, w_down):
    B, S, D = x.shape
    T = B * S
    xf = x.reshape(T, D)
    logits = (xf @ w_router).astype(jnp.float32)
    top_logit, top_e = lax.top_k(logits, MOE_TOPK)
    gates = jax.nn.softmax(top_logit, axis=-1)
    a_exp = top_e.reshape(-1)
    a_tok = jnp.repeat(jnp.arange(T, dtype=jnp.int32), MOE_TOPK)
    a_gate = gates.reshape(-1)
    order = jnp.argsort(a_exp)
    s_exp, s_tok, s_gate = a_exp[order], a_tok[order], a_gate[order]
    counts = jnp.zeros((N_EXPERTS,), jnp.int32).at[a_exp].add(1)
    starts = jnp.cumsum(counts) - counts
    p_counts = (counts + MOE_BLOCK - 1) // MOE_BLOCK * MOE_BLOCK
    p_ends = jnp.cumsum(p_counts)
    p_starts = p_ends - p_counts
    n_assign = MOE_TOPK * T
    dest = p_starts[s_exp] + jnp.arange(n_assign, dtype=jnp.int32) - starts[s_exp]
    n_blocks = -(-n_assign // MOE_BLOCK) + N_EXPERTS
    P = n_blocks * MOE_BLOCK
    buf_tok = jnp.zeros((P,), jnp.int32).at[dest].set(s_tok)
    buf_gate = jnp.zeros((P,), jnp.float32).at[dest].set(s_gate)
    blk_start = jnp.arange(n_blocks, dtype=jnp.int32) * MOE_BLOCK
    blk_exp = jnp.minimum(jnp.searchsorted(p_ends, blk_start, side='right'), N_EXPERTS - 1)

    def expert_block(args):
        e, toks = args
        xb = xf[toks]
        return (jax.nn.silu(xb @ w_gate[e]) * (xb @ w_up[e])) @ w_down[e]

    out = lax.map(expert_block, (blk_exp, buf_tok.reshape(n_blocks, MOE_BLOCK)))
    out = out.reshape(P, D) * buf_gate[:, None].astype(x.dtype)
    y = jnp.zeros((T, D), x.dtype).at[buf_tok].add(out)
    return y.reshape(B, S, D)


def setup_inputs(seed: int = 0) -> dict:
    key = jax.random.key(seed)
    ks = jax.random.split(key, 20)
    f32 = jnp.float32
    n = lambda k, shp, s: jax.random.normal(k, shp, f32) * s
    D = D_MODEL
    w_k = n(ks[3], (D, D), D ** -0.5)
    w_v = n(ks[4], (D, D), D ** -0.5 * DEEPNORM_BETA)
    return {
        'x': jax.random.normal(ks[0], (BATCH, SEQ, D), f32),
        'pool_w': n(ks[1], (N_A_LAYERS, N_POOL_GROUPS, POOL_GROUP_DIM, POOL_GROUP_DIM), POOL_GROUP_DIM ** -0.5 * DEEPNORM_BETA),
        'pool_scale': 1.0 + n(ks[2], (N_A_LAYERS, D), 0.02),
        'w_kv': jnp.concatenate([w_k, w_v], axis=1),
        'moba_wq': n(ks[5], (N_B_LAYERS, D, D), D ** -0.5),
        'moba_wo': n(ks[6], (N_B_LAYERS, D, D), D ** -0.5 * DEEPNORM_BETA),
        'ffn_w_gate': n(ks[7], (N_DENSE, D, D_FF), D ** -0.5),
        'ffn_w_up': n(ks[8], (N_DENSE, D, D_FF), D ** -0.5),
        'ffn_w_down': n(ks[9], (N_DENSE, D_FF, D), D_FF ** -0.5 * DEEPNORM_BETA),
        'moe_router': n(ks[10], (N_MOE, D, N_EXPERTS), D ** -0.5),
        'moe_w_gate': n(ks[11], (N_MOE, N_EXPERTS, D, D_FF_EXPERT), D ** -0.5),
        'moe_w_up': n(ks[12], (N_MOE, N_EXPERTS, D, D_FF_EXPERT), D ** -0.5),
        'moe_w_down': n(ks[13], (N_MOE, N_EXPERTS, D_FF_EXPERT, D), D_FF_EXPERT ** -0.5 * DEEPNORM_BETA),
        'ln_mix_g': 1.0 + n(ks[14], (DEPTH, D), 0.02),
        'ln_mix_b': n(ks[15], (DEPTH, D), 0.02),
        'ln_ffn_g': 1.0 + n(ks[16], (DEPTH, D), 0.02),
        'ln_ffn_b': n(ks[17], (DEPTH, D), 0.02),
    }


def reference(x, pool_w, pool_scale, w_kv, moba_wq, moba_wo, ffn_w_gate, ffn_w_up, ffn_w_down,
              moe_router, moe_w_gate, moe_w_up, moe_w_down, ln_mix_g, ln_mix_b, ln_ffn_g, ln_ffn_b):
    S = x.shape[1]
    cos, sin = rope_tables(S)
    kv = None
    for l in range(DEPTH):
        if l < N_A_LAYERS:
            mix = pool_mixer(x, pool_w[l], pool_scale[l])
        else:
            b = l - N_A_LAYERS
            k_blocks, v_blocks, k_mean = kv
            mix = moba_attention(x, moba_wq[b], moba_wo[b], k_blocks, v_blocks, k_mean, cos, sin)
        x = layer_norm(DEEPNORM_ALPHA * x + mix, ln_mix_g[l], ln_mix_b[l])
        if l % 2 == 0:
            j = l // 2
            f = swiglu(x, ffn_w_gate[j], ffn_w_up[j], ffn_w_down[j])
        else:
            j = l // 2
            f = moe_ffn(x, moe_router[j], moe_w_gate[j], moe_w_up[j], moe_w_down[j])
        x = layer_norm(DEEPNORM_ALPHA * x + f, ln_ffn_g[l], ln_ffn_b[l])
        if l == N_A_LAYERS - 1:
            kv = shared_kv(x, w_kv, cos, sin)
    return x
```

```python
import functools

import jax
import jax.numpy as jnp
from jax import lax
from jax.experimental import pallas as pl
from jax.experimental.pallas import tpu as pltpu

HEAD_DIM = 128
MOBA_BLOCK = 256
MOBA_TOPK = 3
ROPE_THETA = 10000.0
POOL_WINDOWS = (2, 4, 8, 16)
MOE_TOPK = 2
LN_EPS = 1e-5
NEG = -1e30

V7X_LANES = 128
V7X_SUBLANES = 8
V7X_VMEM_LIMIT_BYTES = 56 * 1024 * 1024

F32 = jnp.float32
BF16 = jnp.bfloat16


def _dot(a, b):
    return jnp.dot(a, b, preferred_element_type=F32)


def _dot_nt(a, b):
    return lax.dot_general(a, b, (((1,), (1,)), ((), ())), preferred_element_type=F32)


def _layer_norm(y, g, b):
    mu = jnp.mean(y, axis=-1, keepdims=True)
    d = y - mu
    var = jnp.mean(d * d, axis=-1, keepdims=True)
    return d * lax.rsqrt(var + LN_EPS) * g + b


def _params(semantics):
    return pltpu.CompilerParams(dimension_semantics=semantics, vmem_limit_bytes=V7X_VMEM_LIMIT_BYTES)


def _pick(dim, pref):
    t = min(dim, pref)
    while dim % t:
        t //= 2
    return t


def _ln_rows(x_ref, o_ref, g_ref, b_ref, alpha, rc):
    tm = o_ref.shape[0]
    for c0 in range(0, tm, rc):
        rows = pl.ds(c0, min(rc, tm - c0))
        o_ref[rows, :] = _layer_norm(alpha * x_ref[rows, :] + o_ref[rows, :], g_ref[...], b_ref[...])


def _pool_ln_kernel(xh_ref, x_ref, w_ref, sc_ref, g_ref, b_ref, o_ref, *, alpha, windows, halo):
    i = pl.program_id(1)
    ts, d = x_ref.shape[1], x_ref.shape[2]
    c = d // len(windows)
    t_pos = i * ts + lax.broadcasted_iota(jnp.int32, (ts, 1), 0)
    for g, w in enumerate(windows):
        cols = pl.ds(g * c, c)
        x = x_ref[0, :, cols]
        prev = jnp.where(i > 0, xh_ref[0, :, cols], 0.0)
        cur = jnp.concatenate([prev, x], axis=0)
        width = 1
        while width < w:
            cur = cur + pltpu.roll(cur, width, axis=0)
            width *= 2
        cnt = jnp.minimum(t_pos + 1, w).astype(F32)
        diff = cur[halo:, :] / cnt - x
        y = _dot(diff.astype(BF16), w_ref[g].astype(BF16))
        o_ref[0, :, cols] = y * sc_ref[:, cols]
    _ln_rows(x_ref.at[0], o_ref.at[0], g_ref, b_ref, alpha, 256)


def _pool_ln(x, w_groups, scale, g, b, alpha):
    bsz, s, d = x.shape
    windows = POOL_WINDOWS
    assert len(windows) == w_groups.shape[0] and d % len(windows) == 0
    assert all(w & (w - 1) == 0 for w in windows) and list(windows) == sorted(windows)
    halo = -(-max(windows) // V7X_SUBLANES) * V7X_SUBLANES
    ts = _pick(s, 512)
    assert ts % halo == 0
    r = ts // halo
    kern = functools.partial(_pool_ln_kernel, alpha=alpha, windows=windows, halo=halo)
    return pl.pallas_call(
        kern,
        out_shape=jax.ShapeDtypeStruct(x.shape, F32),
        grid=(bsz, s // ts),
        in_specs=[
            pl.BlockSpec((1, halo, d), lambda bi, i: (bi, jnp.maximum(i * r - 1, 0), 0)),
            pl.BlockSpec((1, ts, d), lambda bi, i: (bi, i, 0)),
            pl.BlockSpec(w_groups.shape, lambda bi, i: (0, 0, 0)),
            pl.BlockSpec((1, d), lambda bi, i: (0, 0)),
            pl.BlockSpec((1, d), lambda bi, i: (0, 0)),
            pl.BlockSpec((1, d), lambda bi, i: (0, 0)),
        ],
        out_specs=pl.BlockSpec((1, ts, d), lambda bi, i: (bi, i, 0)),
        compiler_params=_params(("parallel", "parallel")),
        name="pool_ln",
    )(x, x, w_groups, scale.reshape(1, d), g.reshape(1, d), b.reshape(1, d))


def _ffn_accumulate(xb_ref, wg_ref, wu_ref, wd_ref, wgb_ref, wub_ref, wdb_ref, o_ref, rc):
    wgb_ref[...] = wg_ref[...].astype(BF16)
    wub_ref[...] = wu_ref[...].astype(BF16)
    wdb_ref[...] = wd_ref[...].astype(BF16)
    tm = xb_ref.shape[0]
    for c0 in range(0, tm, rc):
        rows = pl.ds(c0, min(rc, tm - c0))
        xb = xb_ref[rows, :]
        a = _dot(xb, wgb_ref[...])
        u = _dot(xb, wub_ref[...])
        h = (a * jax.nn.sigmoid(a)) * u
        o_ref[rows, :] += _dot(h.astype(BF16), wdb_ref[...])


def _swiglu_ln_kernel(x_ref, wg_ref, wu_ref, wd_ref, g_ref, b_ref, o_ref, xb_ref, wgb_ref, wub_ref, wdb_ref,
                      *, alpha):
    j = pl.program_id(1)

    @pl.when(j == 0)
    def _():
        xb_ref[...] = x_ref[...].astype(BF16)
        o_ref[...] = jnp.zeros_like(o_ref)

    _ffn_accumulate(xb_ref, wg_ref, wu_ref, wd_ref, wgb_ref, wub_ref, wdb_ref, o_ref, 256)

    @pl.when(j == pl.num_programs(1) - 1)
    def _():
        _ln_rows(x_ref, o_ref, g_ref, b_ref, alpha, 256)


def _swiglu_ln(x2, w_gate, w_up, w_down, g, b, alpha):
    t, d = x2.shape
    f = w_gate.shape[1]
    tm = _pick(t, 512)
    tf = _pick(f, 512)
    kern = functools.partial(_swiglu_ln_kernel, alpha=alpha)
    return pl.pallas_call(
        kern,
        out_shape=jax.ShapeDtypeStruct((t, d), F32),
        grid=(t // tm, f // tf),
        in_specs=[
            pl.BlockSpec((tm, d), lambda i, j: (i, 0)),
            pl.BlockSpec((d, tf), lambda i, j: (0, j)),
            pl.BlockSpec((d, tf), lambda i, j: (0, j)),
            pl.BlockSpec((tf, d), lambda i, j: (j, 0)),
            pl.BlockSpec((1, d), lambda i, j: (0, 0)),
            pl.BlockSpec((1, d), lambda i, j: (0, 0)),
        ],
        out_specs=pl.BlockSpec((tm, d), lambda i, j: (i, 0)),
        scratch_shapes=[pltpu.VMEM((tm, d), BF16), pltpu.VMEM((d, tf), BF16), pltpu.VMEM((d, tf), BF16),
                        pltpu.VMEM((tf, d), BF16)],
        compiler_params=_params(("parallel", "arbitrary")),
        name="swiglu_ln",
    )(x2, w_gate, w_up, w_down, g.reshape(1, d), b.reshape(1, d))


def _proj_rope_kernel(x_ref, w_ref, cos_ref, sin_ref, *refs, with_mean, blk):
    if with_mean:
        o_ref, km_ref, xb_ref = refs
    else:
        o_ref, xb_ref = refs
    j = pl.program_id(1)

    @pl.when(j == 0)
    def _():
        xb_ref[...] = x_ref[...].astype(BF16)

    y = _dot(xb_ref[...], w_ref[...].astype(BF16))
    tm, tn = y.shape
    cos = cos_ref[...]
    sin = sin_ref[...]
    heads = []
    for h in range(tn // HEAD_DIM):
        th = y[:, h * HEAD_DIM:(h + 1) * HEAD_DIM]
        heads.append(th * cos + pltpu.roll(th, HEAD_DIM // 2, axis=1) * sin)
    yr = heads[0] if len(heads) == 1 else jnp.concatenate(heads, axis=1)
    o_ref[...] = yr.astype(BF16)
    if with_mean:
        km_ref[0] = jnp.sum(yr.reshape(tm // blk, blk, tn), axis=1) * (1.0 / blk)


def _proj_rope(x2, w, col0, n_out, cos, sin_signed, s, with_mean):
    t, d = x2.shape
    tm = _pick(s, 1024)
    tn = _pick(n_out, 512)
    assert tm % MOBA_BLOCK == 0 and tn % HEAD_DIM == 0 and col0 % tn == 0
    s_tiles = s // tm
    jb = col0 // tn
    out_shape = [jax.ShapeDtypeStruct((t, n_out), BF16)]
    out_specs = [pl.BlockSpec((tm, tn), lambda i, j: (i, j))]
    if with_mean:
        out_shape.append(jax.ShapeDtypeStruct((t // tm, tm // MOBA_BLOCK, n_out), F32))
        out_specs.append(pl.BlockSpec((1, tm // MOBA_BLOCK, tn), lambda i, j: (i, 0, j)))
    kern = functools.partial(_proj_rope_kernel, with_mean=with_mean, blk=MOBA_BLOCK)
    return pl.pallas_call(
        kern,
        out_shape=out_shape,
        grid=(t // tm, n_out // tn),
        in_specs=[
            pl.BlockSpec((tm, d), lambda i, j: (i, 0)),
            pl.BlockSpec((d, tn), lambda i, j: (0, j + jb)),
            pl.BlockSpec((tm, HEAD_DIM), lambda i, j: (i % s_tiles, 0)),
            pl.BlockSpec((tm, HEAD_DIM), lambda i, j: (i % s_tiles, 0)),
        ],
        out_specs=out_specs,
        scratch_shapes=[pltpu.VMEM((tm, d), BF16)],
        compiler_params=_params(("parallel", "arbitrary")),
        name="proj_rope_k" if with_mean else "proj_rope_q",
    )(x2, w, cos, sin_signed)


def _proj_vt_kernel(x_ref, w_ref, o_ref, xb_ref, *, blk):
    j = pl.program_id(1)

    @pl.when(j == 0)
    def _():
        xb_ref[...] = x_ref[...].astype(BF16)

    y = _dot(xb_ref[...], w_ref[...].astype(BF16))
    tm, tn = y.shape
    yt = y.T
    for h in range(tn // HEAD_DIM):
        for c in range(tm // blk):
            o_ref[0, h, c] = yt[h * HEAD_DIM:(h + 1) * HEAD_DIM, c * blk:(c + 1) * blk].astype(BF16)


def _proj_vt(x2, w, col0, n_out, bsz, s):
    t, d = x2.shape
    tm = _pick(s, 512)
    tn = _pick(n_out, 256)
    assert tm % MOBA_BLOCK == 0 and tn % HEAD_DIM == 0 and col0 % tn == 0
    s_tiles = s // tm
    jb = col0 // tn
    hpt = tn // HEAD_DIM
    nbt = tm // MOBA_BLOCK
    kern = functools.partial(_proj_vt_kernel, blk=MOBA_BLOCK)
    return pl.pallas_call(
        kern,
        out_shape=jax.ShapeDtypeStruct((bsz, n_out // HEAD_DIM, s // MOBA_BLOCK, HEAD_DIM, MOBA_BLOCK), BF16),
        grid=(t // tm, n_out // tn),
        in_specs=[
            pl.BlockSpec((tm, d), lambda i, j: (i, 0)),
            pl.BlockSpec((d, tn), lambda i, j: (0, j + jb)),
        ],
        out_specs=pl.BlockSpec((1, hpt, nbt, HEAD_DIM, MOBA_BLOCK),
                               lambda i, j: (i // s_tiles, j, i % s_tiles, 0, 0)),
        scratch_shapes=[pltpu.VMEM((tm, d), BF16)],
        compiler_params=_params(("parallel", "arbitrary")),
        name="proj_vt",
    )(x2, w)


def _moba_kernel(q_ref, k_ref, vt_ref, km_ref, o_ref, bias_ref, *, scale, topk):
    qi = pl.program_id(2)
    blk = q_ref.shape[1]
    nb = km_ref.shape[1]
    q = q_ref[0]

    gate = _dot_nt(km_ref[0].astype(BF16), q)
    blk_id = lax.broadcasted_iota(jnp.int32, (nb, blk), 0)
    past = blk_id < qi
    gate = jnp.where(past, gate, NEG)
    rank = jnp.zeros((nb, blk), jnp.int32)
    for m in range(nb):
        gm = gate[m:m + 1, :]
        better = (gm > gate) | ((gm == gate) & (blk_id > m))
        rank = rank + better.astype(jnp.int32)
    bias_ref[...] = jnp.where(past & (rank < topk), 0.0, NEG)

    k_own = k_ref[0, pl.ds(pl.multiple_of(qi * blk, blk), blk), :]
    s = _dot_nt(k_own, q) * scale
    kpos = lax.broadcasted_iota(jnp.int32, (blk, blk), 0)
    qpos = lax.broadcasted_iota(jnp.int32, (blk, blk), 1)
    s = jnp.where(kpos <= qpos, s, NEG)
    m0 = jnp.max(s, axis=0, keepdims=True)
    p = jnp.exp(s - m0)
    l0 = jnp.sum(p, axis=0, keepdims=True)
    acc0 = _dot(vt_ref[0, 0, qi], p.astype(BF16))

    def body(n, carry):
        m_i, l_i, acc = carry
        k_n = k_ref[0, pl.ds(pl.multiple_of(n * blk, blk), blk), :]
        s_n = _dot_nt(k_n, q) * scale + bias_ref[pl.ds(n, 1), :]
        m_new = jnp.maximum(m_i, jnp.max(s_n, axis=0, keepdims=True))
        a = jnp.exp(m_i - m_new)
        p_n = jnp.exp(s_n - m_new)
        l_new = a * l_i + jnp.sum(p_n, axis=0, keepdims=True)
        acc_new = acc * a + _dot(vt_ref[0, 0, n], p_n.astype(BF16))
        return m_new, l_new, acc_new

    _, l_f, acc_f = lax.fori_loop(0, qi, body, (m0, l0, acc0))
    o_ref[0] = (acc_f / l_f).T.astype(BF16)


def _moba(q, k, vt, k_mean):
    bsz, s, d = q.shape
    n_heads = d // HEAD_DIM
    nb = s // MOBA_BLOCK
    kern = functools.partial(_moba_kernel, scale=HEAD_DIM ** -0.5, topk=min(MOBA_TOPK, nb))
    return pl.pallas_call(
        kern,
        out_shape=jax.ShapeDtypeStruct((bsz, s, d), BF16),
        grid=(bsz, n_heads, nb),
        in_specs=[
            pl.BlockSpec((1, MOBA_BLOCK, HEAD_DIM), lambda b, h, i: (b, i, h)),
            pl.BlockSpec((1, s, HEAD_DIM), lambda b, h, i: (b, 0, h)),
            pl.BlockSpec((1, 1, nb, HEAD_DIM, MOBA_BLOCK), lambda b, h, i: (b, h, 0, 0, 0)),
            pl.BlockSpec((1, nb, HEAD_DIM), lambda b, h, i: (b, 0, h)),
        ],
        out_specs=pl.BlockSpec((1, MOBA_BLOCK, HEAD_DIM), lambda b, h, i: (b, i, h)),
        scratch_shapes=[pltpu.VMEM((nb, MOBA_BLOCK), F32)],
        compiler_params=_params(("parallel", "parallel", "arbitrary")),
        name="moba",
    )(q, k, vt, k_mean)


def _oproj_ln_kernel(a_ref, w_ref, x_ref, g_ref, b_ref, *refs, alpha, n_exp):
    if n_exp:
        wr_ref, o_ref, idx_ref, gate_ref = refs
    else:
        (o_ref,) = refs
    kk = pl.program_id(1)
    tm = o_ref.shape[0]
    rc = min(tm, 256)

    @pl.when(kk == 0)
    def _():
        o_ref[...] = jnp.zeros_like(o_ref)

    wb = w_ref[...].astype(BF16)
    for c0 in range(0, tm, rc):
        rows = pl.ds(c0, rc)
        o_ref[rows, :] += _dot(a_ref[rows, :], wb)

    @pl.when(kk == pl.num_programs(1) - 1)
    def _():
        for c0 in range(0, tm, rc):
            rows = pl.ds(c0, rc)
            y = _layer_norm(alpha * x_ref[rows, :] + o_ref[rows, :], g_ref[...], b_ref[...])
            o_ref[rows, :] = y
            if n_exp:
                logits = _dot(y.astype(BF16), wr_ref[...].astype(BF16))
                lane = lax.broadcasted_iota(jnp.int32, logits.shape, 1)
                big = logits.shape[1]
                logits = jnp.where(lane < n_exp, logits, -jnp.inf)
                m1 = jnp.max(logits, axis=1, keepdims=True)
                i1 = jnp.min(jnp.where(logits == m1, lane, big), axis=1, keepdims=True)
                rest = jnp.where(lane == i1, -jnp.inf, logits)
                m2 = jnp.max(rest, axis=1, keepdims=True)
                i2 = jnp.min(jnp.where(rest == m2, lane, big), axis=1, keepdims=True)
                e2 = jnp.exp(m2 - m1)
                den = 1.0 + e2
                idx_ref[rows, :] = jnp.where(lane == 0, i1, jnp.where(lane == 1, i2, 0))
                gate_ref[rows, :] = jnp.where(lane == 0, 1.0 / den, jnp.where(lane == 1, e2 / den, 0.0))


def _oproj_ln(a2, w_o, x2, g, b, alpha, w_router=None):
    t, d = x2.shape
    kdim = a2.shape[1]
    tm = _pick(t, 512)
    tk = _pick(kdim, 512)
    n_exp = 0 if w_router is None else w_router.shape[1]
    in_specs = [
        pl.BlockSpec((tm, tk), lambda i, k: (i, k)),
        pl.BlockSpec((tk, d), lambda i, k: (k, 0)),
        pl.BlockSpec((tm, d), lambda i, k: (i, 0)),
        pl.BlockSpec((1, d), lambda i, k: (0, 0)),
        pl.BlockSpec((1, d), lambda i, k: (0, 0)),
    ]
    args = [a2, w_o, x2, g.reshape(1, d), b.reshape(1, d)]
    out_shape = [jax.ShapeDtypeStruct((t, d), F32)]
    out_specs = [pl.BlockSpec((tm, d), lambda i, k: (i, 0))]
    if n_exp:
        assert MOE_TOPK == 2 and n_exp <= V7X_LANES
        wr = jnp.pad(w_router, ((0, 0), (0, V7X_LANES - n_exp)))
        in_specs.append(pl.BlockSpec((d, V7X_LANES), lambda i, k: (0, 0)))
        args.append(wr)
        out_shape += [jax.ShapeDtypeStruct((t, V7X_LANES), jnp.int32), jax.ShapeDtypeStruct((t, V7X_LANES), F32)]
        out_specs += [pl.BlockSpec((tm, V7X_LANES), lambda i, k: (i, 0))] * 2
    kern = functools.partial(_oproj_ln_kernel, alpha=alpha, n_exp=n_exp)
    return pl.pallas_call(
        kern,
        out_shape=out_shape,
        grid=(t // tm, kdim // tk),
        in_specs=in_specs,
        out_specs=out_specs,
        compiler_params=_params(("parallel", "arbitrary")),
        name="oproj_ln",
    )(*args)


def _route(top_e, gates, tm, n_exp):
    t = top_e.shape[0]
    n_assign = t * MOE_TOPK
    a_exp = top_e.reshape(-1)
    onehot = (a_exp[:, None] == jnp.arange(n_exp, dtype=jnp.int32)[None, :]).astype(jnp.int32)
    csum = jnp.cumsum(onehot, axis=0)
    rank = jnp.sum(csum * onehot, axis=1) - 1
    counts = csum[-1]
    p_counts = (counts + tm - 1) // tm * tm
    p_ends = jnp.cumsum(p_counts)
    p_starts = p_ends - p_counts
    dest = (p_starts[a_exp] + rank).astype(jnp.int32)
    n_blocks = -(-n_assign // tm) + n_exp
    p_rows = n_blocks * tm
    a_tok = jnp.arange(n_assign, dtype=jnp.int32) // MOE_TOPK
    buf_tok = jnp.zeros((p_rows,), jnp.int32).at[dest].set(a_tok)
    buf_gate = jnp.zeros((p_rows,), F32).at[dest].set(gates.reshape(-1))
    blk_start = jnp.arange(n_blocks, dtype=jnp.int32) * tm
    blk_exp = jnp.minimum(jnp.searchsorted(p_ends, blk_start, side="right"), n_exp - 1).astype(jnp.int32)
    n_used = (p_ends[-1] // tm).astype(jnp.int32).reshape(1)
    return buf_tok, buf_gate, blk_exp, n_used, dest.reshape(t, MOE_TOPK)


def _row_copy(src_hbm, row, dst_vmem, r, sem):
    return pltpu.make_async_copy(src_hbm.at[pl.ds(row, 1)], dst_vmem.at[pl.ds(r, 1)], sem)


def _moe_kernel(exp_ref, used_ref, tok_ref, x_hbm, gate_ref, wg_ref, wu_ref, wd_ref, o_ref,
                xg_ref, xb_ref, wgb_ref, wub_ref, wdb_ref, sem):
    i = pl.program_id(0)
    j = pl.program_id(1)
    tm = xg_ref.shape[0]
    active = i < used_ref[0]

    @pl.when(j == 0)
    def _():
        o_ref[...] = jnp.zeros_like(o_ref)

    @pl.when(jnp.logical_and(active, j == 0))
    def _():
        def start(r, c):
            _row_copy(x_hbm, tok_ref[i * tm + r], xg_ref, r, sem).start()
            return c

        def wait(r, c):
            _row_copy(x_hbm, 0, xg_ref, r, sem).wait()
            return c

        lax.fori_loop(0, tm, start, 0)
        lax.fori_loop(0, tm, wait, 0)
        xb_ref[...] = xg_ref[...].astype(BF16)

    @pl.when(active)
    def _():
        _ffn_accumulate(xb_ref, wg_ref.at[0], wu_ref.at[0], wd_ref.at[0], wgb_ref, wub_ref, wdb_ref, o_ref, 256)

        @pl.when(j == pl.num_programs(1) - 1)
        def _():
            o_ref[...] = o_ref[...] * gate_ref[...]


def _moe(x2, buf_tok, buf_gate, blk_exp, n_used, w_gate, w_up, w_down, tm):
    t, d = x2.shape
    n_exp, _, f = w_gate.shape
    tf = _pick(f, 512)
    n_blocks = blk_exp.shape[0]
    nj = f // tf

    def w_idx(i, j, exp_ref, used_ref, tok_ref):
        last = used_ref[0] - 1
        return exp_ref[jnp.minimum(i, last)], jnp.where(i <= last, j, nj - 1)

    def wgu_map(i, j, *s):
        e, jj = w_idx(i, j, *s)
        return e, 0, jj

    def wd_map(i, j, *s):
        e, jj = w_idx(i, j, *s)
        return e, jj, 0

    grid_spec = pltpu.PrefetchScalarGridSpec(
        num_scalar_prefetch=3,
        grid=(n_blocks, nj),
        in_specs=[
            pl.BlockSpec(memory_space=pl.ANY),
            pl.BlockSpec((tm, 1), lambda i, j, *s: (i, 0)),
            pl.BlockSpec((1, d, tf), wgu_map),
            pl.BlockSpec((1, d, tf), wgu_map),
            pl.BlockSpec((1, tf, d), wd_map),
        ],
        out_specs=pl.BlockSpec((tm, d), lambda i, j, *s: (i, 0)),
        scratch_shapes=[pltpu.VMEM((tm, d), F32), pltpu.VMEM((tm, d), BF16), pltpu.VMEM((d, tf), BF16),
                        pltpu.VMEM((d, tf), BF16), pltpu.VMEM((tf, d), BF16), pltpu.SemaphoreType.DMA],
    )
    return pl.pallas_call(
        _moe_kernel,
        out_shape=jax.ShapeDtypeStruct((n_blocks * tm, d), F32),
        grid_spec=grid_spec,
        compiler_params=_params(("arbitrary", "arbitrary")),
        name="moe",
    )(blk_exp, n_used, buf_tok, x2, buf_gate.reshape(-1, 1), w_gate, w_up, w_down)


def _combine_ln_kernel(p0_ref, p1_ref, x_ref, y_hbm, g_ref, b_ref, o_ref, ybuf_ref, sems, *, alpha):
    i = pl.program_id(0)
    tm = x_ref.shape[0]
    slots = ((p0_ref, 0), (p1_ref, 1))

    def start(r, c):
        for p_ref, k in slots:
            _row_copy(y_hbm, p_ref[i * tm + r], ybuf_ref.at[k], r, sems.at[k]).start()
        return c

    def wait(r, c):
        for _, k in slots:
            _row_copy(y_hbm, 0, ybuf_ref.at[k], r, sems.at[k]).wait()
        return c

    lax.fori_loop(0, tm, start, 0)
    lax.fori_loop(0, tm, wait, 0)
    y = ybuf_ref[0] + ybuf_ref[1]
    o_ref[...] = _layer_norm(alpha * x_ref[...] + y, g_ref[...], b_ref[...])


def _combine_ln(x2, y_sorted, pos, g, b, alpha):
    t, d = x2.shape
    tm = _pick(t, 256)
    grid_spec = pltpu.PrefetchScalarGridSpec(
        num_scalar_prefetch=2,
        grid=(t // tm,),
        in_specs=[
            pl.BlockSpec((tm, d), lambda i, *s: (i, 0)),
            pl.BlockSpec(memory_space=pl.ANY),
            pl.BlockSpec((1, d), lambda i, *s: (0, 0)),
            pl.BlockSpec((1, d), lambda i, *s: (0, 0)),
        ],
        out_specs=pl.BlockSpec((tm, d), lambda i, *s: (i, 0)),
        scratch_shapes=[pltpu.VMEM((MOE_TOPK, tm, d), F32), pltpu.SemaphoreType.DMA((MOE_TOPK,))],
    )
    kern = functools.partial(_combine_ln_kernel, alpha=alpha)
    return pl.pallas_call(
        kern,
        out_shape=jax.ShapeDtypeStruct((t, d), F32),
        grid_spec=grid_spec,
        compiler_params=_params(("arbitrary",)),
        name="combine_ln",
    )(pos[:, 0], pos[:, 1], x2, y_sorted, g.reshape(1, d), b.reshape(1, d))


def _rope_tables(s):
    inv = 1.0 / (ROPE_THETA ** (jnp.arange(0, HEAD_DIM, 2, dtype=F32) / HEAD_DIM))
    ang = jnp.arange(s, dtype=F32)[:, None] * inv[None, :]
    ang = jnp.concatenate([ang, ang], axis=-1)
    sign = jnp.where(jnp.arange(HEAD_DIM) < HEAD_DIM // 2, -1.0, 1.0).astype(F32)
    return jnp.cos(ang), jnp.sin(ang) * sign[None, :]


def kernel(x, pool_w, pool_scale, w_kv, moba_wq, moba_wo, ffn_w_gate, ffn_w_up, ffn_w_down,
           moe_router, moe_w_gate, moe_w_up, moe_w_down, ln_mix_g, ln_mix_b, ln_ffn_g, ln_ffn_b):
    bsz, s, d = x.shape
    t = bsz * s
    depth = ln_mix_g.shape[0]
    n_a = pool_w.shape[0]
    alpha = (2.0 * depth) ** 0.25
    cos, sin_signed = _rope_tables(s)
    nb = s // MOBA_BLOCK
    assert s % MOBA_BLOCK == 0 and d % HEAD_DIM == 0
    kv = None
    x2 = x.reshape(t, d)
    for l in range(depth):
        if l < n_a:
            x2 = _pool_ln(x2.reshape(bsz, s, d), pool_w[l], pool_scale[l], ln_mix_g[l], ln_mix_b[l],
                          alpha).reshape(t, d)
            router = None
        else:
            bl = l - n_a
            k, vt, k_mean = kv
            q = _proj_rope(x2, moba_wq[bl], 0, d, cos, sin_signed, s, with_mean=False)[0]
            att = _moba(q.reshape(bsz, s, d), k.reshape(bsz, s, d), vt, k_mean)
            router = moe_router[l // 2] if l % 2 == 1 else None
            res = _oproj_ln(att.reshape(t, d), moba_wo[bl], x2, ln_mix_g[l], ln_mix_b[l], alpha, router)
            x2 = res[0]
        jf = l // 2
        if l % 2 == 0:
            x2 = _swiglu_ln(x2, ffn_w_gate[jf], ffn_w_up[jf], ffn_w_down[jf], ln_ffn_g[l], ln_ffn_b[l], alpha)
        else:
            n_exp = moe_router.shape[2]
            if router is None:
                raise NotImplementedError("MoE after a pooling mixer is not supported")
            top_e = res[1][:, :MOE_TOPK]
            gates = res[2][:, :MOE_TOPK]
            tm = _pick(t * MOE_TOPK, 512)
            buf_tok, buf_gate, blk_exp, n_used, pos = _route(top_e, gates, tm, n_exp)
            y_sorted = _moe(x2, buf_tok, buf_gate, blk_exp, n_used,
                            moe_w_gate[jf], moe_w_up[jf], moe_w_down[jf], tm)
            x2 = _combine_ln(x2, y_sorted, pos, ln_ffn_g[l], ln_ffn_b[l], alpha)
        if l == n_a - 1:
            k, k_mean = _proj_rope(x2, w_kv, 0, d, cos, sin_signed, s, with_mean=True)
            vt = _proj_vt(x2, w_kv, d, d, bsz, s)
            kv = (k, vt, k_mean.reshape(bsz, nb, d))
    return x2.reshape(bsz, s, d)
```

```python
import functools

import jax
import jax.numpy as jnp
from jax import lax
from jax.experimental import pallas as pl
from jax.experimental.pallas import tpu as pltpu

HEAD_DIM = 128
MOBA_BLOCK = 256
MOBA_TOPK = 3
ROPE_THETA = 10000.0
POOL_WINDOWS = (2, 4, 8, 16)
MOE_TOPK = 2
LN_EPS = 1e-5
NEG = -1e30

V7X_LANES = 128
V7X_SUBLANES = 8
V7X_VMEM_LIMIT_BYTES = 56 * 1024 * 1024

F32 = jnp.float32
BF16 = jnp.bfloat16


def _dot(a, b):
    return jnp.dot(a, b, preferred_element_type=F32)


def _dot_nt(a, b):
    return lax.dot_general(a, b, (((1,), (1,)), ((), ())), preferred_element_type=F32)


def _layer_norm(y, g, b):
    mu = jnp.mean(y, axis=-1, keepdims=True)
    d = y - mu
    var = jnp.mean(d * d, axis=-1, keepdims=True)
    return d * lax.rsqrt(var + LN_EPS) * g + b


def _params(semantics):
    return pltpu.CompilerParams(dimension_semantics=semantics, vmem_limit_bytes=V7X_VMEM_LIMIT_BYTES)


def _pick(dim, pref):
    t = min(dim, pref)
    while dim % t:
        t //= 2
    return t


def _ln_rows(x_ref, o_ref, g_ref, b_ref, alpha, rc):
    tm = o_ref.shape[0]
    for c0 in range(0, tm, rc):
        rows = pl.ds(c0, min(rc, tm - c0))
        o_ref[rows, :] = _layer_norm(alpha * x_ref[rows, :] + o_ref[rows, :], g_ref[...], b_ref[...])


def _pool_ln_kernel(xh_ref, x_ref, w_ref, sc_ref, g_ref, b_ref, o_ref, *, alpha, windows, halo):
    i = pl.program_id(1)
    ts, d = x_ref.shape[1], x_ref.shape[2]
    c = d // len(windows)
    t_pos = i * ts + lax.broadcasted_iota(jnp.int32, (ts, 1), 0)
    for g, w in enumerate(windows):
        cols = pl.ds(g * c, c)
        x = x_ref[0, :, cols]
        prev = jnp.where(i > 0, xh_ref[0, :, cols], 0.0)
        cur = jnp.concatenate([prev, x], axis=0)
        width = 1
        while width < w:
            cur = cur + pltpu.roll(cur, width, axis=0)
            width *= 2
        cnt = jnp.minimum(t_pos + 1, w).astype(F32)
        diff = cur[halo:, :] / cnt - x
        y = _dot(diff.astype(BF16), w_ref[g].astype(BF16))
        o_ref[0, :, cols] = y * sc_ref[:, cols]
    _ln_rows(x_ref.at[0], o_ref.at[0], g_ref, b_ref, alpha, 256)


def _pool_ln(x, w_groups, scale, g, b, alpha):
    bsz, s, d = x.shape
    windows = POOL_WINDOWS
    assert len(windows) == w_groups.shape[0] and d % len(windows) == 0
    assert all(w & (w - 1) == 0 for w in windows) and list(windows) == sorted(windows)
    halo = -(-max(windows) // V7X_SUBLANES) * V7X_SUBLANES
    ts = _pick(s, 512)
    assert ts % halo == 0
    r = ts // halo
    kern = functools.partial(_pool_ln_kernel, alpha=alpha, windows=windows, halo=halo)
    return pl.pallas_call(
        kern,
        out_shape=jax.ShapeDtypeStruct(x.shape, F32),
        grid=(bsz, s // ts),
        in_specs=[
            pl.BlockSpec((1, halo, d), lambda bi, i: (bi, jnp.maximum(i * r - 1, 0), 0)),
            pl.BlockSpec((1, ts, d), lambda bi, i: (bi, i, 0)),
            pl.BlockSpec(w_groups.shape, lambda bi, i: (0, 0, 0)),
            pl.BlockSpec((1, d), lambda bi, i: (0, 0)),
            pl.BlockSpec((1, d), lambda bi, i: (0, 0)),
            pl.BlockSpec((1, d), lambda bi, i: (0, 0)),
        ],
        out_specs=pl.BlockSpec((1, ts, d), lambda bi, i: (bi, i, 0)),
        compiler_params=_params(("parallel", "parallel")),
        name="pool_ln",
    )(x, x, w_groups, scale.reshape(1, d), g.reshape(1, d), b.reshape(1, d))


def _ffn_accumulate(xb_ref, wg_ref, wu_ref, wd_ref, wgb_ref, wub_ref, wdb_ref, o_ref, rc, n_rows=None):
    wgb_ref[...] = wg_ref[...].astype(BF16)
    wub_ref[...] = wu_ref[...].astype(BF16)
    wdb_ref[...] = wd_ref[...].astype(BF16)
    tm = xb_ref.shape[0]
    for c0 in range(0, tm, rc):
        rows = pl.ds(c0, min(rc, tm - c0))

        def chunk(rows=rows):
            xb = xb_ref[rows, :]
            a = _dot(xb, wgb_ref[...])
            u = _dot(xb, wub_ref[...])
            h = (a * jax.nn.sigmoid(a)) * u
            o_ref[rows, :] += _dot(h.astype(BF16), wdb_ref[...])

        if n_rows is None or c0 == 0:
            chunk()
        else:
            pl.when(c0 < n_rows)(chunk)


def _swiglu_ln_kernel(x_ref, wg_ref, wu_ref, wd_ref, g_ref, b_ref, o_ref, xb_ref, wgb_ref, wub_ref, wdb_ref,
                      *, alpha):
    j = pl.program_id(1)

    @pl.when(j == 0)
    def _():
        xb_ref[...] = x_ref[...].astype(BF16)
        o_ref[...] = jnp.zeros_like(o_ref)

    _ffn_accumulate(xb_ref, wg_ref, wu_ref, wd_ref, wgb_ref, wub_ref, wdb_ref, o_ref, 256)

    @pl.when(j == pl.num_programs(1) - 1)
    def _():
        _ln_rows(x_ref, o_ref, g_ref, b_ref, alpha, 256)


def _swiglu_ln(x2, w_gate, w_up, w_down, g, b, alpha):
    t, d = x2.shape
    f = w_gate.shape[1]
    tm = _pick(t, 1024)
    tf = _pick(f, 256)
    kern = functools.partial(_swiglu_ln_kernel, alpha=alpha)
    return pl.pallas_call(
        kern,
        out_shape=jax.ShapeDtypeStruct((t, d), F32),
        grid=(t // tm, f // tf),
        in_specs=[
            pl.BlockSpec((tm, d), lambda i, j: (i, 0), pipeline_mode=pl.Buffered(1)),
            pl.BlockSpec((d, tf), lambda i, j: (0, j)),
            pl.BlockSpec((d, tf), lambda i, j: (0, j)),
            pl.BlockSpec((tf, d), lambda i, j: (j, 0)),
            pl.BlockSpec((1, d), lambda i, j: (0, 0)),
            pl.BlockSpec((1, d), lambda i, j: (0, 0)),
        ],
        out_specs=pl.BlockSpec((tm, d), lambda i, j: (i, 0)),
        scratch_shapes=[pltpu.VMEM((tm, d), BF16), pltpu.VMEM((d, tf), BF16), pltpu.VMEM((d, tf), BF16),
                        pltpu.VMEM((tf, d), BF16)],
        compiler_params=_params(("parallel", "arbitrary")),
        name="swiglu_ln",
    )(x2, w_gate, w_up, w_down, g.reshape(1, d), b.reshape(1, d))


def _proj_rope_kernel(x_ref, w_ref, cos_ref, sin_ref, *refs, with_mean, blk):
    if with_mean:
        o_ref, km_ref, xb_ref = refs
    else:
        o_ref, xb_ref = refs
    j = pl.program_id(1)

    @pl.when(j == 0)
    def _():
        xb_ref[...] = x_ref[...].astype(BF16)

    y = _dot(xb_ref[...], w_ref[...].astype(BF16))
    tm, tn = y.shape
    cos = cos_ref[...]
    sin = sin_ref[...]
    heads = []
    for h in range(tn // HEAD_DIM):
        th = y[:, h * HEAD_DIM:(h + 1) * HEAD_DIM]
        heads.append(th * cos + pltpu.roll(th, HEAD_DIM // 2, axis=1) * sin)
    yr = heads[0] if len(heads) == 1 else jnp.concatenate(heads, axis=1)
    o_ref[...] = yr.astype(BF16)
    if with_mean:
        km_ref[0] = jnp.sum(yr.reshape(tm // blk, blk, tn), axis=1) * (1.0 / blk)


def _proj_rope(x2, w, col0, n_out, cos, sin_signed, s, with_mean):
    t, d = x2.shape
    tm = _pick(s, 1024)
    tn = _pick(n_out, 512)
    assert tm % MOBA_BLOCK == 0 and tn % HEAD_DIM == 0 and col0 % tn == 0
    s_tiles = s // tm
    jb = col0 // tn
    out_shape = [jax.ShapeDtypeStruct((t, n_out), BF16)]
    out_specs = [pl.BlockSpec((tm, tn), lambda i, j: (i, j))]
    if with_mean:
        out_shape.append(jax.ShapeDtypeStruct((t // tm, tm // MOBA_BLOCK, n_out), F32))
        out_specs.append(pl.BlockSpec((1, tm // MOBA_BLOCK, tn), lambda i, j: (i, 0, j)))
    kern = functools.partial(_proj_rope_kernel, with_mean=with_mean, blk=MOBA_BLOCK)
    return pl.pallas_call(
        kern,
        out_shape=out_shape,
        grid=(t // tm, n_out // tn),
        in_specs=[
            pl.BlockSpec((tm, d), lambda i, j: (i, 0)),
            pl.BlockSpec((d, tn), lambda i, j: (0, j + jb)),
            pl.BlockSpec((tm, HEAD_DIM), lambda i, j: (i % s_tiles, 0)),
            pl.BlockSpec((tm, HEAD_DIM), lambda i, j: (i % s_tiles, 0)),
        ],
        out_specs=out_specs,
        scratch_shapes=[pltpu.VMEM((tm, d), BF16)],
        compiler_params=_params(("parallel", "arbitrary")),
        name="proj_rope_k" if with_mean else "proj_rope_q",
    )(x2, w, cos, sin_signed)


def _proj_vt_kernel(x_ref, w_ref, o_ref, xb_ref, *, blk):
    j = pl.program_id(1)

    @pl.when(j == 0)
    def _():
        xb_ref[...] = x_ref[...].astype(BF16)

    y = _dot(xb_ref[...], w_ref[...].astype(BF16))
    tm, tn = y.shape
    yt = y.T
    for h in range(tn // HEAD_DIM):
        for c in range(tm // blk):
            o_ref[0, h, c] = yt[h * HEAD_DIM:(h + 1) * HEAD_DIM, c * blk:(c + 1) * blk].astype(BF16)


def _proj_vt(x2, w, col0, n_out, bsz, s):
    t, d = x2.shape
    tm = _pick(s, 512)
    tn = _pick(n_out, 256)
    assert tm % MOBA_BLOCK == 0 and tn % HEAD_DIM == 0 and col0 % tn == 0
    s_tiles = s // tm
    jb = col0 // tn
    hpt = tn // HEAD_DIM
    nbt = tm // MOBA_BLOCK
    kern = functools.partial(_proj_vt_kernel, blk=MOBA_BLOCK)
    return pl.pallas_call(
        kern,
        out_shape=jax.ShapeDtypeStruct((bsz, n_out // HEAD_DIM, s // MOBA_BLOCK, HEAD_DIM, MOBA_BLOCK), BF16),
        grid=(t // tm, n_out // tn),
        in_specs=[
            pl.BlockSpec((tm, d), lambda i, j: (i, 0)),
            pl.BlockSpec((d, tn), lambda i, j: (0, j + jb)),
        ],
        out_specs=pl.BlockSpec((1, hpt, nbt, HEAD_DIM, MOBA_BLOCK),
                               lambda i, j: (i // s_tiles, j, i % s_tiles, 0, 0)),
        scratch_shapes=[pltpu.VMEM((tm, d), BF16)],
        compiler_params=_params(("parallel", "arbitrary")),
        name="proj_vt",
    )(x2, w)


def _moba_kernel(q_ref, k_ref, vt_ref, km_ref, o_ref, bias_ref, s_ref, m8_ref, m_ref, l8_ref, acc_ref,
                 *, scale, topk, heads):
    qi = pl.program_id(2)
    blk = q_ref.shape[1]
    nb = km_ref.shape[1]
    blk_id = lax.broadcasted_iota(jnp.int32, (nb, blk), 0)
    past = blk_id < qi
    kpos = lax.broadcasted_iota(jnp.int32, (blk, blk), 0)
    qpos = lax.broadcasted_iota(jnp.int32, (blk, blk), 1)
    own_rows = pl.ds(pl.multiple_of(qi * blk, blk), blk)

    def fold(x, op):
        return op(x.reshape(blk // V7X_SUBLANES, V7X_SUBLANES, blk), axis=0)

    for h in range(heads):
        hs = pl.ds(h * HEAD_DIM, HEAD_DIM)
        q = q_ref[0, :, hs]
        gate = _dot_nt(km_ref[0, :, hs].astype(BF16), q)
        gate = jnp.where(past, gate, NEG)
        rank = jnp.zeros((nb, blk), jnp.int32)
        for m in range(nb):
            gm = gate[m:m + 1, :]
            better = (gm > gate) | ((gm == gate) & (blk_id > m))
            rank = rank + better.astype(jnp.int32)
        bias_ref[h] = jnp.where(past & (rank < topk), 0.0, NEG)

        s = _dot_nt(k_ref[0, own_rows, hs], q) * scale
        s = jnp.where(kpos <= qpos, s, NEG)
        s_ref[h, qi] = s
        m8_ref[h] = fold(s, jnp.max)

    def scores(n, carry):
        rows = pl.ds(pl.multiple_of(n * blk, blk), blk)
        for h in range(heads):
            hs = pl.ds(h * HEAD_DIM, HEAD_DIM)
            s_n = _dot_nt(k_ref[0, rows, hs], q_ref[0, :, hs]) * scale + bias_ref[h, pl.ds(n, 1), :]
            s_ref[h, n] = s_n
            m8_ref[h] = jnp.maximum(m8_ref[h], fold(s_n, jnp.max))
        return carry

    lax.fori_loop(0, qi, scores, 0)

    for h in range(heads):
        m_ref[h] = jnp.max(m8_ref[h], axis=0, keepdims=True)
        l8_ref[h] = jnp.zeros(l8_ref.shape[1:], F32)
        acc_ref[h] = jnp.zeros(acc_ref.shape[1:], F32)

    def values(n, carry):
        for h in range(heads):
            p_n = jnp.exp(s_ref[h, n] - m_ref[h])
            l8_ref[h] += fold(p_n, jnp.sum)
            acc_ref[h] += _dot(vt_ref[0, h, n], p_n.astype(BF16))
        return carry

    lax.fori_loop(0, qi + 1, values, 0)
    for h in range(heads):
        l = jnp.sum(l8_ref[h], axis=0, keepdims=True)
        o_ref[0, :, pl.ds(h * HEAD_DIM, HEAD_DIM)] = (acc_ref[h] / l).T.astype(BF16)


def _moba(q, k, vt, k_mean):
    bsz, s, d = q.shape
    n_heads = d // HEAD_DIM
    nb = s // MOBA_BLOCK
    heads = _pick(n_heads, 4)
    gw = heads * HEAD_DIM
    kern = functools.partial(_moba_kernel, scale=HEAD_DIM ** -0.5, topk=min(MOBA_TOPK, nb), heads=heads)
    return pl.pallas_call(
        kern,
        out_shape=jax.ShapeDtypeStruct((bsz, s, d), BF16),
        grid=(bsz, n_heads // heads, nb),
        in_specs=[
            pl.BlockSpec((1, MOBA_BLOCK, gw), lambda b, h, i: (b, i, h)),
            pl.BlockSpec((1, s, gw), lambda b, h, i: (b, 0, h)),
            pl.BlockSpec((1, heads, nb, HEAD_DIM, MOBA_BLOCK), lambda b, h, i: (b, h, 0, 0, 0)),
            pl.BlockSpec((1, nb, gw), lambda b, h, i: (b, 0, h)),
        ],
        out_specs=pl.BlockSpec((1, MOBA_BLOCK, gw), lambda b, h, i: (b, i, h)),
        scratch_shapes=[pltpu.VMEM((heads, nb, MOBA_BLOCK), F32),
                        pltpu.VMEM((heads, nb, MOBA_BLOCK, MOBA_BLOCK), F32),
                        pltpu.VMEM((heads, V7X_SUBLANES, MOBA_BLOCK), F32),
                        pltpu.VMEM((heads, 1, MOBA_BLOCK), F32),
                        pltpu.VMEM((heads, V7X_SUBLANES, MOBA_BLOCK), F32),
                        pltpu.VMEM((heads, HEAD_DIM, MOBA_BLOCK), F32)],
        compiler_params=_params(("parallel", "parallel", "arbitrary")),
        name="moba",
    )(q, k, vt, k_mean)


def _oproj_ln_kernel(a_ref, w_ref, x_ref, g_ref, b_ref, *refs, alpha, n_exp):
    if n_exp:
        wr_ref, o_ref, idx_ref, gate_ref = refs
    else:
        (o_ref,) = refs
    kk = pl.program_id(1)
    tm = o_ref.shape[0]
    rc = min(tm, 256)

    @pl.when(kk == 0)
    def _():
        o_ref[...] = jnp.zeros_like(o_ref)

    wb = w_ref[...].astype(BF16)
    for c0 in range(0, tm, rc):
        rows = pl.ds(c0, rc)
        o_ref[rows, :] += _dot(a_ref[rows, :], wb)

    @pl.when(kk == pl.num_programs(1) - 1)
    def _():
        for c0 in range(0, tm, rc):
            rows = pl.ds(c0, rc)
            y = _layer_norm(alpha * x_ref[rows, :] + o_ref[rows, :], g_ref[...], b_ref[...])
            o_ref[rows, :] = y
            if n_exp:
                logits = _dot(y.astype(BF16), wr_ref[...].astype(BF16))
                lane = lax.broadcasted_iota(jnp.int32, logits.shape, 1)
                big = logits.shape[1]
                logits = jnp.where(lane < n_exp, logits, -jnp.inf)
                m1 = jnp.max(logits, axis=1, keepdims=True)
                i1 = jnp.min(jnp.where(logits == m1, lane, big), axis=1, keepdims=True)
                rest = jnp.where(lane == i1, -jnp.inf, logits)
                m2 = jnp.max(rest, axis=1, keepdims=True)
                i2 = jnp.min(jnp.where(rest == m2, lane, big), axis=1, keepdims=True)
                e2 = jnp.exp(m2 - m1)
                den = 1.0 + e2
                idx_ref[rows, :] = jnp.where(lane == 0, i1, jnp.where(lane == 1, i2, 0))
                gate_ref[rows, :] = jnp.where(lane == 0, 1.0 / den, jnp.where(lane == 1, e2 / den, 0.0))


def _oproj_ln(a2, w_o, x2, g, b, alpha, w_router=None):
    t, d = x2.shape
    kdim = a2.shape[1]
    tm = _pick(t, 512)
    tk = _pick(kdim, 512)
    n_exp = 0 if w_router is None else w_router.shape[1]
    in_specs = [
        pl.BlockSpec((tm, tk), lambda i, k: (i, k)),
        pl.BlockSpec((tk, d), lambda i, k: (k, 0)),
        pl.BlockSpec((tm, d), lambda i, k: (i, 0)),
        pl.BlockSpec((1, d), lambda i, k: (0, 0)),
        pl.BlockSpec((1, d), lambda i, k: (0, 0)),
    ]
    args = [a2, w_o, x2, g.reshape(1, d), b.reshape(1, d)]
    out_shape = [jax.ShapeDtypeStruct((t, d), F32)]
    out_specs = [pl.BlockSpec((tm, d), lambda i, k: (i, 0))]
    if n_exp:
        assert MOE_TOPK == 2 and n_exp <= V7X_LANES
        wr = jnp.pad(w_router, ((0, 0), (0, V7X_LANES - n_exp)))
        in_specs.append(pl.BlockSpec((d, V7X_LANES), lambda i, k: (0, 0)))
        args.append(wr)
        out_shape += [jax.ShapeDtypeStruct((t, V7X_LANES), jnp.int32), jax.ShapeDtypeStruct((t, V7X_LANES), F32)]
        out_specs += [pl.BlockSpec((tm, V7X_LANES), lambda i, k: (i, 0))] * 2
    kern = functools.partial(_oproj_ln_kernel, alpha=alpha, n_exp=n_exp)
    return pl.pallas_call(
        kern,
        out_shape=out_shape,
        grid=(t // tm, kdim // tk),
        in_specs=in_specs,
        out_specs=out_specs,
        compiler_params=_params(("parallel", "arbitrary")),
        name="oproj_ln",
    )(*args)


def _route(top_e, gates, tm, n_exp):
    t = top_e.shape[0]
    n_assign = t * MOE_TOPK
    a_exp = top_e.reshape(-1)
    onehot = (a_exp[:, None] == jnp.arange(n_exp, dtype=jnp.int32)[None, :]).astype(jnp.int32)
    csum = jnp.cumsum(onehot, axis=0)
    rank = jnp.sum(csum * onehot, axis=1) - 1
    counts = csum[-1]
    p_counts = (counts + tm - 1) // tm * tm
    p_ends = jnp.cumsum(p_counts)
    p_starts = p_ends - p_counts
    dest = (p_starts[a_exp] + rank).astype(jnp.int32)
    n_blocks = -(-n_assign // tm) + n_exp
    p_rows = n_blocks * tm
    a_tok = jnp.arange(n_assign, dtype=jnp.int32) // MOE_TOPK
    buf_tok = jnp.zeros((p_rows,), jnp.int32).at[dest].set(a_tok)
    buf_gate = jnp.zeros((p_rows,), F32).at[dest].set(gates.reshape(-1))
    blk_start = jnp.arange(n_blocks, dtype=jnp.int32) * tm
    blk_exp = jnp.minimum(jnp.searchsorted(p_ends, blk_start, side="right"), n_exp - 1).astype(jnp.int32)
    blk_rows = jnp.clip(p_starts[blk_exp] + counts[blk_exp] - blk_start, 0, tm).astype(jnp.int32)
    n_used = (p_ends[-1] // tm).astype(jnp.int32).reshape(1)
    return buf_tok, buf_gate, blk_exp, blk_rows, n_used, dest.reshape(t, MOE_TOPK)


def _row_copy(src_hbm, row, dst_vmem, r, sem):
    return pltpu.make_async_copy(src_hbm.at[pl.ds(row, 1)], dst_vmem.at[pl.ds(r, 1)], sem)


def _moe_kernel(exp_ref, rows_ref, used_ref, tok_ref, x_hbm, gate_ref, wg_ref, wu_ref, wd_ref, o_ref,
                xg_ref, xb_ref, wgb_ref, wub_ref, wdb_ref, sem, *, rc):
    i = pl.program_id(0)
    j = pl.program_id(1)
    tm = xg_ref.shape[0]
    n_used = used_ref[0]
    active = i < n_used

    def start_gather(tile):
        for c0 in range(0, tm, rc):
            @pl.when(c0 < rows_ref[tile])
            def _(c0=c0):
                def start(r, c):
                    _row_copy(x_hbm, tok_ref[tile * tm + c0 + r], xg_ref, c0 + r, sem).start()
                    return c

                lax.fori_loop(0, rc, start, 0, unroll=8)

    @pl.when(jnp.logical_and(i == 0, j == 0))
    def _():
        start_gather(0)

    @pl.when(jnp.logical_and(j == 1, i + 1 < n_used))
    def _():
        start_gather(i + 1)

    @pl.when(j == 0)
    def _():
        o_ref[...] = jnp.zeros_like(o_ref)

    @pl.when(jnp.logical_and(active, j == 0))
    def _():
        for c0 in range(0, tm, rc):
            @pl.when(c0 < rows_ref[i])
            def _(c0=c0):
                rows = pl.ds(c0, rc)
                pltpu.make_async_copy(x_hbm.at[rows], xg_ref.at[rows], sem).wait()
        for c0 in range(0, tm, rc):
            @pl.when(c0 < rows_ref[i])
            def _(c0=c0):
                rows = pl.ds(c0, rc)
                xb_ref[rows, :] = xg_ref[rows, :].astype(BF16)

    @pl.when(active)
    def _():
        _ffn_accumulate(xb_ref, wg_ref.at[0], wu_ref.at[0], wd_ref.at[0], wgb_ref, wub_ref, wdb_ref, o_ref, rc,
                        n_rows=rows_ref[i])

        @pl.when(j == pl.num_programs(1) - 1)
        def _():
            for c0 in range(0, tm, rc):
                rows = pl.ds(c0, rc)
                o_ref[rows, :] = o_ref[rows, :] * gate_ref[rows, :]


def _moe(x2, buf_tok, buf_gate, blk_exp, blk_rows, n_used, w_gate, w_up, w_down, tm):
    t, d = x2.shape
    n_exp, _, f = w_gate.shape
    tf = _pick(f, 256)
    rc = _pick(tm, 256)
    n_blocks = blk_exp.shape[0]
    nj = f // tf
    assert nj >= 2

    def w_idx(i, j, exp_ref, rows_ref, used_ref, tok_ref):
        last = used_ref[0] - 1
        return exp_ref[jnp.minimum(i, last)], jnp.where(i <= last, j, nj - 1)

    def wgu_map(i, j, *s):
        e, jj = w_idx(i, j, *s)
        return e, 0, jj

    def wd_map(i, j, *s):
        e, jj = w_idx(i, j, *s)
        return e, jj, 0

    grid_spec = pltpu.PrefetchScalarGridSpec(
        num_scalar_prefetch=4,
        grid=(n_blocks, nj),
        in_specs=[
            pl.BlockSpec(memory_space=pl.ANY),
            pl.BlockSpec((tm, 1), lambda i, j, *s: (i, 0)),
            pl.BlockSpec((1, d, tf), wgu_map),
            pl.BlockSpec((1, d, tf), wgu_map),
            pl.BlockSpec((1, tf, d), wd_map),
        ],
        out_specs=pl.BlockSpec((tm, d), lambda i, j, *s: (i, 0)),
        scratch_shapes=[pltpu.VMEM((tm, d), F32), pltpu.VMEM((tm, d), BF16), pltpu.VMEM((d, tf), BF16),
                        pltpu.VMEM((d, tf), BF16), pltpu.VMEM((tf, d), BF16), pltpu.SemaphoreType.DMA],
    )
    return pl.pallas_call(
        functools.partial(_moe_kernel, rc=rc),
        out_shape=jax.ShapeDtypeStruct((n_blocks * tm, d), F32),
        grid_spec=grid_spec,
        compiler_params=_params(("arbitrary", "arbitrary")),
        name="moe",
    )(blk_exp, blk_rows, n_used, buf_tok, x2, buf_gate.reshape(-1, 1), w_gate, w_up, w_down)


def _combine_ln_kernel(p0_ref, p1_ref, x_ref, y_hbm, g_ref, b_ref, o_ref, ybuf_ref, sems, *, alpha):
    i = pl.program_id(0)
    tm = x_ref.shape[0]
    slot = i % 2

    def start_gather(tile, sl):
        def start(r, c):
            for k, p_ref in enumerate((p0_ref, p1_ref)):
                _row_copy(y_hbm, p_ref[tile * tm + r], ybuf_ref.at[sl, k], r, sems.at[sl, k]).start()
            return c

        lax.fori_loop(0, tm, start, 0, unroll=8)

    @pl.when(i == 0)
    def _():
        start_gather(0, 0)

    @pl.when(i + 1 < pl.num_programs(0))
    def _():
        start_gather(i + 1, 1 - slot)

    for k in range(MOE_TOPK):
        pltpu.make_async_copy(y_hbm.at[pl.ds(0, tm)], ybuf_ref.at[slot, k], sems.at[slot, k]).wait()
    y = ybuf_ref[slot, 0] + ybuf_ref[slot, 1]
    o_ref[...] = _layer_norm(alpha * x_ref[...] + y, g_ref[...], b_ref[...])


def _combine_ln(x2, y_sorted, pos, g, b, alpha):
    t, d = x2.shape
    tm = _pick(t, 256)
    grid_spec = pltpu.PrefetchScalarGridSpec(
        num_scalar_prefetch=2,
        grid=(t // tm,),
        in_specs=[
            pl.BlockSpec((tm, d), lambda i, *s: (i, 0)),
            pl.BlockSpec(memory_space=pl.ANY),
            pl.BlockSpec((1, d), lambda i, *s: (0, 0)),
            pl.BlockSpec((1, d), lambda i, *s: (0, 0)),
        ],
        out_specs=pl.BlockSpec((tm, d), lambda i, *s: (i, 0)),
        scratch_shapes=[pltpu.VMEM((2, MOE_TOPK, tm, d), F32), pltpu.SemaphoreType.DMA((2, MOE_TOPK))],
    )
    kern = functools.partial(_combine_ln_kernel, alpha=alpha)
    return pl.pallas_call(
        kern,
        out_shape=jax.ShapeDtypeStruct((t, d), F32),
        grid_spec=grid_spec,
        compiler_params=_params(("arbitrary",)),
        name="combine_ln",
    )(pos[:, 0], pos[:, 1], x2, y_sorted, g.reshape(1, d), b.reshape(1, d))


def _rope_tables(s):
    inv = 1.0 / (ROPE_THETA ** (jnp.arange(0, HEAD_DIM, 2, dtype=F32) / HEAD_DIM))
    ang = jnp.arange(s, dtype=F32)[:, None] * inv[None, :]
    ang = jnp.concatenate([ang, ang], axis=-1)
    sign = jnp.where(jnp.arange(HEAD_DIM) < HEAD_DIM // 2, -1.0, 1.0).astype(F32)
    return jnp.cos(ang), jnp.sin(ang) * sign[None, :]


def kernel(x, pool_w, pool_scale, w_kv, moba_wq, moba_wo, ffn_w_gate, ffn_w_up, ffn_w_down,
           moe_router, moe_w_gate, moe_w_up, moe_w_down, ln_mix_g, ln_mix_b, ln_ffn_g, ln_ffn_b):
    bsz, s, d = x.shape
    t = bsz * s
    depth = ln_mix_g.shape[0]
    n_a = pool_w.shape[0]
    alpha = (2.0 * depth) ** 0.25
    cos, sin_signed = _rope_tables(s)
    nb = s // MOBA_BLOCK
    assert s % MOBA_BLOCK == 0 and d % HEAD_DIM == 0
    kv = None
    x2 = x.reshape(t, d)
    for l in range(depth):
        if l < n_a:
            x2 = _pool_ln(x2.reshape(bsz, s, d), pool_w[l], pool_scale[l], ln_mix_g[l], ln_mix_b[l],
                          alpha).reshape(t, d)
            router = None
        else:
            bl = l - n_a
            k, vt, k_mean = kv
            q = _proj_rope(x2, moba_wq[bl], 0, d, cos, sin_signed, s, with_mean=False)[0]
            att = _moba(q.reshape(bsz, s, d), k.reshape(bsz, s, d), vt, k_mean)
            router = moe_router[l // 2] if l % 2 == 1 else None
            res = _oproj_ln(att.reshape(t, d), moba_wo[bl], x2, ln_mix_g[l], ln_mix_b[l], alpha, router)
            x2 = res[0]
        jf = l // 2
        if l % 2 == 0:
            x2 = _swiglu_ln(x2, ffn_w_gate[jf], ffn_w_up[jf], ffn_w_down[jf], ln_ffn_g[l], ln_ffn_b[l], alpha)
        else:
            n_exp = moe_router.shape[2]
            if router is None:
                raise NotImplementedError("MoE after a pooling mixer is not supported")
            top_e = res[1][:, :MOE_TOPK]
            gates = res[2][:, :MOE_TOPK]
            tm = _pick(t * MOE_TOPK, 1024)
            buf_tok, buf_gate, blk_exp, blk_rows, n_used, pos = _route(top_e, gates, tm, n_exp)
            y_sorted = _moe(x2, buf_tok, buf_gate, blk_exp, blk_rows, n_used,
                            moe_w_gate[jf], moe_w_up[jf], moe_w_down[jf], tm)
            x2 = _combine_ln(x2, y_sorted, pos, ln_ffn_g[l], ln_ffn_b[l], alpha)
        if l == n_a - 1:
            k, k_mean = _proj_rope(x2, w_kv, 0, d, cos, sin_signed, s, with_mean=True)
            vt = _proj_vt(x2, w_kv, d, d, bsz, s)
            kv = (k, vt, k_mean.reshape(bsz, nb, d))
    return x2.reshape(bsz, s, d)
```

```python
import functools

import jax
import jax.numpy as jnp
from jax import lax
from jax.experimental import pallas as pl
from jax.experimental.pallas import tpu as pltpu

HEAD_DIM = 128
MOBA_BLOCK = 256
MOBA_TOPK = 3
ROPE_THETA = 10000.0
POOL_WINDOWS = (2, 4, 8, 16)
MOE_TOPK = 2
LN_EPS = 1e-5
NEG = -1e30

V7X_LANES = 128
V7X_SUBLANES = 8
V7X_VMEM_LIMIT_BYTES = 56 * 1024 * 1024

F32 = jnp.float32
BF16 = jnp.bfloat16


def _dot(a, b):
    return jnp.dot(a, b, preferred_element_type=F32)


def _dot_nt(a, b):
    return lax.dot_general(a, b, (((1,), (1,)), ((), ())), preferred_element_type=F32)


def _layer_norm(y, g, b):
    mu = jnp.mean(y, axis=-1, keepdims=True)
    d = y - mu
    var = jnp.mean(d * d, axis=-1, keepdims=True)
    return d * lax.rsqrt(var + LN_EPS) * g + b


def _params(semantics):
    return pltpu.CompilerParams(dimension_semantics=semantics, vmem_limit_bytes=V7X_VMEM_LIMIT_BYTES)


def _pick(dim, pref):
    t = min(dim, pref)
    while dim % t:
        t //= 2
    return t


def _ln_rows(x_ref, o_ref, g_ref, b_ref, alpha, rc):
    tm = o_ref.shape[0]
    for c0 in range(0, tm, rc):
        rows = pl.ds(c0, min(rc, tm - c0))
        o_ref[rows, :] = _layer_norm(alpha * x_ref[rows, :] + o_ref[rows, :], g_ref[...], b_ref[...])


def _pool_ln_kernel(xh_ref, x_ref, w_ref, sc_ref, g_ref, b_ref, o_ref, *, alpha, windows, halo):
    i = pl.program_id(1)
    ts, d = x_ref.shape[1], x_ref.shape[2]
    c = d // len(windows)
    t_pos = i * ts + lax.broadcasted_iota(jnp.int32, (ts, 1), 0)
    for g, w in enumerate(windows):
        cols = pl.ds(g * c, c)
        x = x_ref[0, :, cols]
        prev = jnp.where(i > 0, xh_ref[0, :, cols], 0.0)
        cur = jnp.concatenate([prev, x], axis=0)
        width = 1
        while width < w:
            cur = cur + pltpu.roll(cur, width, axis=0)
            width *= 2
        cnt = jnp.minimum(t_pos + 1, w).astype(F32)
        diff = cur[halo:, :] / cnt - x
        y = _dot(diff.astype(BF16), w_ref[g].astype(BF16))
        o_ref[0, :, cols] = y * sc_ref[:, cols]
    _ln_rows(x_ref.at[0], o_ref.at[0], g_ref, b_ref, alpha, 256)


def _pool_ln(x, w_groups, scale, g, b, alpha):
    bsz, s, d = x.shape
    windows = POOL_WINDOWS
    assert len(windows) == w_groups.shape[0] and d % len(windows) == 0
    assert all(w & (w - 1) == 0 for w in windows) and list(windows) == sorted(windows)
    halo = -(-max(windows) // V7X_SUBLANES) * V7X_SUBLANES
    ts = _pick(s, 512)
    assert ts % halo == 0
    r = ts // halo
    kern = functools.partial(_pool_ln_kernel, alpha=alpha, windows=windows, halo=halo)
    return pl.pallas_call(
        kern,
        out_shape=jax.ShapeDtypeStruct(x.shape, F32),
        grid=(bsz, s // ts),
        in_specs=[
            pl.BlockSpec((1, halo, d), lambda bi, i: (bi, jnp.maximum(i * r - 1, 0), 0)),
            pl.BlockSpec((1, ts, d), lambda bi, i: (bi, i, 0)),
            pl.BlockSpec(w_groups.shape, lambda bi, i: (0, 0, 0)),
            pl.BlockSpec((1, d), lambda bi, i: (0, 0)),
            pl.BlockSpec((1, d), lambda bi, i: (0, 0)),
            pl.BlockSpec((1, d), lambda bi, i: (0, 0)),
        ],
        out_specs=pl.BlockSpec((1, ts, d), lambda bi, i: (bi, i, 0)),
        compiler_params=_params(("parallel", "parallel")),
        name="pool_ln",
    )(x, x, w_groups, scale.reshape(1, d), g.reshape(1, d), b.reshape(1, d))


def _ffn_accumulate(xb_ref, wg_ref, wu_ref, wd_ref, wgb_ref, wub_ref, wdb_ref, o_ref, rc, n_rows=None, rc_part=None):
    wgb_ref[...] = wg_ref[...].astype(BF16)
    wub_ref[...] = wu_ref[...].astype(BF16)
    wdb_ref[...] = wd_ref[...].astype(BF16)
    tm = xb_ref.shape[0]

    def chunk(c0, size):
        rows = pl.ds(c0, size)
        xb = xb_ref[rows, :]
        a = _dot(xb, wgb_ref[...])
        u = _dot(xb, wub_ref[...])
        h = (a * jax.nn.sigmoid(a)) * u
        o_ref[rows, :] += _dot(h.astype(BF16), wdb_ref[...])

    def full():
        for c0 in range(0, tm, rc):
            chunk(c0, rc)

    if n_rows is None:
        full()
        return
    pl.when(n_rows == tm)(full)

    @pl.when(n_rows < tm)
    def _():
        for c0 in range(0, tm, rc_part):
            pl.when(c0 < n_rows)(functools.partial(chunk, c0, rc_part))


def _swiglu_ln_kernel(x_ref, wg_ref, wu_ref, wd_ref, g_ref, b_ref, o_ref, ob_ref, wgb_ref, wub_ref, wdb_ref,
                      *, alpha):
    j = pl.program_id(1)

    @pl.when(j == 0)
    def _():
        ob_ref[...] = x_ref[...].astype(BF16)
        o_ref[...] = jnp.zeros_like(o_ref)

    _ffn_accumulate(ob_ref, wg_ref, wu_ref, wd_ref, wgb_ref, wub_ref, wdb_ref, o_ref, _pick(o_ref.shape[0], 512))

    @pl.when(j == pl.num_programs(1) - 1)
    def _():
        _ln_rows(x_ref, o_ref, g_ref, b_ref, alpha, 256)
        ob_ref[...] = o_ref[...].astype(BF16)


def _swiglu_ln(x2, w_gate, w_up, w_down, g, b, alpha):
    t, d = x2.shape
    f = w_gate.shape[1]
    tm = _pick(t, 1024)
    tf = _pick(f, 256)
    kern = functools.partial(_swiglu_ln_kernel, alpha=alpha)
    return pl.pallas_call(
        kern,
        out_shape=[jax.ShapeDtypeStruct((t, d), F32), jax.ShapeDtypeStruct((t, d), BF16)],
        grid=(t // tm, f // tf),
        in_specs=[
            pl.BlockSpec((tm, d), lambda i, j: (i, 0), pipeline_mode=pl.Buffered(1)),
            pl.BlockSpec((d, tf), lambda i, j: (0, j)),
            pl.BlockSpec((d, tf), lambda i, j: (0, j)),
            pl.BlockSpec((tf, d), lambda i, j: (j, 0)),
            pl.BlockSpec((1, d), lambda i, j: (0, 0)),
            pl.BlockSpec((1, d), lambda i, j: (0, 0)),
        ],
        out_specs=[pl.BlockSpec((tm, d), lambda i, j: (i, 0)), pl.BlockSpec((tm, d), lambda i, j: (i, 0))],
        scratch_shapes=[pltpu.VMEM((d, tf), BF16), pltpu.VMEM((d, tf), BF16), pltpu.VMEM((tf, d), BF16)],
        compiler_params=_params(("parallel", "arbitrary")),
        name="swiglu_ln",
    )(x2, w_gate, w_up, w_down, g.reshape(1, d), b.reshape(1, d))


def _to_bf16_kernel(w_ref, o_ref):
    o_ref[...] = w_ref[...].astype(BF16)


def _to_bf16(w):
    r, c = w.shape
    tr = _pick(r, 512)
    return pl.pallas_call(
        _to_bf16_kernel,
        out_shape=jax.ShapeDtypeStruct((r, c), BF16),
        grid=(r // tr,),
        in_specs=[pl.BlockSpec((tr, c), lambda i: (i, 0))],
        out_specs=pl.BlockSpec((tr, c), lambda i: (i, 0)),
        compiler_params=_params(("parallel",)),
        name="to_bf16",
    )(w)


def _proj_rope_kernel(x_ref, w_ref, cos_ref, sin_ref, o_ref, *km_ref, blk):
    tm, tn = o_ref.shape
    for c0 in range(0, tm, blk):
        rows = pl.ds(c0, blk)
        y = _dot(x_ref[rows, :], w_ref[...])
        cos = cos_ref[rows, :]
        sin = sin_ref[rows, :]
        heads = []
        for h in range(tn // HEAD_DIM):
            th = y[:, h * HEAD_DIM:(h + 1) * HEAD_DIM]
            heads.append(th * cos + pltpu.roll(th, HEAD_DIM // 2, axis=1) * sin)
        yr = heads[0] if len(heads) == 1 else jnp.concatenate(heads, axis=1)
        o_ref[rows, :] = yr.astype(BF16)
        if km_ref:
            km_ref[0][0, pl.ds(c0 // blk, 1), :] = jnp.sum(yr, axis=0, keepdims=True) * (1.0 / blk)


def _proj_rope(xb, wb, col0, n_out, cos, sin_signed, s, with_mean):
    t, d = xb.shape
    tm = _pick(s, 1024)
    tn = _pick(n_out, 1024)
    assert tm % MOBA_BLOCK == 0 and tn % HEAD_DIM == 0 and col0 % tn == 0
    s_tiles = s // tm
    jb = col0 // tn
    out_shape = [jax.ShapeDtypeStruct((t, n_out), BF16)]
    out_specs = [pl.BlockSpec((tm, tn), lambda i, j: (i, j))]
    if with_mean:
        out_shape.append(jax.ShapeDtypeStruct((t // tm, tm // MOBA_BLOCK, n_out), F32))
        out_specs.append(pl.BlockSpec((1, tm // MOBA_BLOCK, tn), lambda i, j: (i, 0, j)))
    kern = functools.partial(_proj_rope_kernel, blk=MOBA_BLOCK)
    return pl.pallas_call(
        kern,
        out_shape=out_shape,
        grid=(t // tm, n_out // tn),
        in_specs=[
            pl.BlockSpec((tm, d), lambda i, j: (i, 0)),
            pl.BlockSpec((d, tn), lambda i, j: (0, j + jb)),
            pl.BlockSpec((tm, HEAD_DIM), lambda i, j: (i % s_tiles, 0)),
            pl.BlockSpec((tm, HEAD_DIM), lambda i, j: (i % s_tiles, 0)),
        ],
        out_specs=out_specs,
        compiler_params=_params(("parallel", "arbitrary")),
        name="proj_rope_k" if with_mean else "proj_rope_q",
    )(xb, wb, cos, sin_signed)


def _proj_vt_kernel(x_ref, w_ref, o_ref, *, blk):
    tm = x_ref.shape[0]
    tn = w_ref.shape[1]
    for c in range(tm // blk):
        yt = _dot(x_ref[pl.ds(c * blk, blk), :], w_ref[...]).T
        for h in range(tn // HEAD_DIM):
            o_ref[0, h, c] = yt[h * HEAD_DIM:(h + 1) * HEAD_DIM, :].astype(BF16)


def _proj_vt(xb, wb, col0, n_out, bsz, s):
    t, d = xb.shape
    tm = _pick(s, 1024)
    tn = _pick(n_out, 1024)
    assert tm % MOBA_BLOCK == 0 and tn % HEAD_DIM == 0 and col0 % tn == 0
    s_tiles = s // tm
    jb = col0 // tn
    hpt = tn // HEAD_DIM
    nbt = tm // MOBA_BLOCK
    kern = functools.partial(_proj_vt_kernel, blk=MOBA_BLOCK)
    return pl.pallas_call(
        kern,
        out_shape=jax.ShapeDtypeStruct((bsz, n_out // HEAD_DIM, s // MOBA_BLOCK, HEAD_DIM, MOBA_BLOCK), BF16),
        grid=(t // tm, n_out // tn),
        in_specs=[
            pl.BlockSpec((tm, d), lambda i, j: (i, 0)),
            pl.BlockSpec((d, tn), lambda i, j: (0, j + jb)),
        ],
        out_specs=pl.BlockSpec((1, hpt, nbt, HEAD_DIM, MOBA_BLOCK),
                               lambda i, j: (i // s_tiles, j, i % s_tiles, 0, 0)),
        compiler_params=_params(("parallel", "arbitrary")),
        name="proj_vt",
    )(xb, wb)


def _moba_kernel(q_ref, k_ref, vt_ref, km_ref, o_ref, bias_ref, s_ref, m8_ref, m_ref, l8_ref, acc_ref,
                 *, scale, topk, heads, unroll):
    qi = pl.program_id(2)
    blk = q_ref.shape[1]
    nb = km_ref.shape[1]
    blk_id = lax.broadcasted_iota(jnp.int32, (nb, blk), 0)
    past = blk_id < qi
    kpos = lax.broadcasted_iota(jnp.int32, (blk, blk), 0)
    qpos = lax.broadcasted_iota(jnp.int32, (blk, blk), 1)
    own_rows = pl.ds(pl.multiple_of(qi * blk, blk), blk)

    def fold(x, op):
        return op(x.reshape(blk // V7X_SUBLANES, V7X_SUBLANES, blk), axis=0)

    for h in range(heads):
        hs = pl.ds(h * HEAD_DIM, HEAD_DIM)
        q = q_ref[0, :, hs]
        gate = _dot_nt(km_ref[0, :, hs].astype(BF16), q)
        gate = jnp.where(past, gate, NEG)
        rank = jnp.zeros((nb, blk), jnp.int32)
        for m in range(nb):
            gm = gate[m:m + 1, :]
            better = (gm > gate) | ((gm == gate) & (blk_id > m))
            rank = rank + better.astype(jnp.int32)
        bias_ref[h] = jnp.where(past & (rank < topk), 0.0, NEG)

        m8_ref[h] = jnp.full(m8_ref.shape[1:], NEG, F32)

    def scores(it, carry):
        for un in range(unroll):
            n = it * unroll + un
            nk = jnp.minimum(n, nb - 1)
            rows = pl.ds(pl.multiple_of(nk * blk, blk), blk)
            for h in range(heads):
                hs = pl.ds(h * HEAD_DIM, HEAD_DIM)
                s_n = _dot_nt(k_ref[0, rows, hs], q_ref[0, :, hs]) * scale + bias_ref[h, pl.ds(nk, 1), :]
                s_ref[h, n] = s_n
                m8_ref[h] = jnp.maximum(m8_ref[h], fold(s_n, jnp.max))
        return carry

    lax.fori_loop(0, (qi + unroll - 1) // unroll, scores, 0)

    for h in range(heads):
        hs = pl.ds(h * HEAD_DIM, HEAD_DIM)
        s = _dot_nt(k_ref[0, own_rows, hs], q_ref[0, :, hs]) * scale
        s = jnp.where(kpos <= qpos, s, NEG)
        s_ref[h, qi] = s
        for un in range(1, unroll):
            s_ref[h, qi + un] = jnp.full((blk, blk), NEG, F32)
        m_ref[h] = jnp.max(jnp.maximum(m8_ref[h], fold(s, jnp.max)), axis=0, keepdims=True)
        l8_ref[h] = jnp.zeros(l8_ref.shape[1:], F32)
        acc_ref[h] = jnp.zeros(acc_ref.shape[1:], F32)

    def values(it, carry):
        for un in range(unroll):
            n = it * unroll + un
            nv = jnp.minimum(n, nb - 1)
            for h in range(heads):
                p_n = jnp.exp(s_ref[h, n] - m_ref[h])
                l8_ref[h] += fold(p_n, jnp.sum)
                acc_ref[h] += _dot(vt_ref[0, h, nv], p_n.astype(BF16))
        return carry

    lax.fori_loop(0, (qi + unroll) // unroll, values, 0)
    for h in range(heads):
        l = jnp.sum(l8_ref[h], axis=0, keepdims=True)
        o_ref[0, :, pl.ds(h * HEAD_DIM, HEAD_DIM)] = (acc_ref[h] / l).T.astype(BF16)


def _moba(q, k, vt, k_mean):
    bsz, s, d = q.shape
    n_heads = d // HEAD_DIM
    nb = s // MOBA_BLOCK
    heads = _pick(n_heads, 4)
    unroll = 2
    gw = heads * HEAD_DIM
    kern = functools.partial(_moba_kernel, scale=HEAD_DIM ** -0.5, topk=min(MOBA_TOPK, nb), heads=heads,
                             unroll=unroll)
    return pl.pallas_call(
        kern,
        out_shape=jax.ShapeDtypeStruct((bsz, s, d), BF16),
        grid=(bsz, n_heads // heads, nb),
        in_specs=[
            pl.BlockSpec((1, MOBA_BLOCK, gw), lambda b, h, i: (b, i, h)),
            pl.BlockSpec((1, s, gw), lambda b, h, i: (b, 0, h)),
            pl.BlockSpec((1, heads, nb, HEAD_DIM, MOBA_BLOCK), lambda b, h, i: (b, h, 0, 0, 0)),
            pl.BlockSpec((1, nb, gw), lambda b, h, i: (b, 0, h)),
        ],
        out_specs=pl.BlockSpec((1, MOBA_BLOCK, gw), lambda b, h, i: (b, i, h)),
        scratch_shapes=[pltpu.VMEM((heads, nb, MOBA_BLOCK), F32),
                        pltpu.VMEM((heads, nb + unroll - 1, MOBA_BLOCK, MOBA_BLOCK), F32),
                        pltpu.VMEM((heads, V7X_SUBLANES, MOBA_BLOCK), F32),
                        pltpu.VMEM((heads, 1, MOBA_BLOCK), F32),
                        pltpu.VMEM((heads, V7X_SUBLANES, MOBA_BLOCK), F32),
                        pltpu.VMEM((heads, HEAD_DIM, MOBA_BLOCK), F32)],
        compiler_params=_params(("parallel", "parallel", "arbitrary")),
        name="moba",
    )(q, k, vt, k_mean)


def _oproj_ln_kernel(a_ref, w_ref, x_ref, g_ref, b_ref, *refs, alpha, n_exp):
    if n_exp:
        wr_ref, o_ref, idx_ref, gate_ref = refs
    else:
        (o_ref,) = refs
    kk = pl.program_id(1)
    tm = o_ref.shape[0]
    rc = min(tm, 256)

    @pl.when(kk == 0)
    def _():
        o_ref[...] = jnp.zeros_like(o_ref)

    for c0 in range(0, tm, rc):
        rows = pl.ds(c0, rc)
        o_ref[rows, :] += _dot(a_ref[rows, :], w_ref[...])

    @pl.when(kk == pl.num_programs(1) - 1)
    def _():
        for c0 in range(0, tm, rc):
            rows = pl.ds(c0, rc)
            y = _layer_norm(alpha * x_ref[rows, :] + o_ref[rows, :], g_ref[...], b_ref[...])
            o_ref[rows, :] = y
            if n_exp:
                logits = _dot(y.astype(BF16), wr_ref[...].astype(BF16))
                lane = lax.broadcasted_iota(jnp.int32, logits.shape, 1)
                big = logits.shape[1]
                logits = jnp.where(lane < n_exp, logits, -jnp.inf)
                m1 = jnp.max(logits, axis=1, keepdims=True)
                i1 = jnp.min(jnp.where(logits == m1, lane, big), axis=1, keepdims=True)
                rest = jnp.where(lane == i1, -jnp.inf, logits)
                m2 = jnp.max(rest, axis=1, keepdims=True)
                i2 = jnp.min(jnp.where(rest == m2, lane, big), axis=1, keepdims=True)
                e2 = jnp.exp(m2 - m1)
                den = 1.0 + e2
                idx_ref[rows, :] = jnp.where(lane == 0, i1, jnp.where(lane == 1, i2, 0))
                gate_ref[rows, :] = jnp.where(lane == 0, 1.0 / den, jnp.where(lane == 1, e2 / den, 0.0))


def _oproj_ln(a2, w_o, x2, g, b, alpha, w_router=None):
    t, d = x2.shape
    kdim = a2.shape[1]
    tm = _pick(t, 1024)
    tk = _pick(kdim, 1024)
    n_exp = 0 if w_router is None else w_router.shape[1]
    in_specs = [
        pl.BlockSpec((tm, tk), lambda i, k: (i, k)),
        pl.BlockSpec((tk, d), lambda i, k: (k, 0)),
        pl.BlockSpec((tm, d), lambda i, k: (i, 0), pipeline_mode=pl.Buffered(1)),
        pl.BlockSpec((1, d), lambda i, k: (0, 0)),
        pl.BlockSpec((1, d), lambda i, k: (0, 0)),
    ]
    args = [a2, w_o, x2, g.reshape(1, d), b.reshape(1, d)]
    out_shape = [jax.ShapeDtypeStruct((t, d), F32)]
    out_specs = [pl.BlockSpec((tm, d), lambda i, k: (i, 0))]
    if n_exp:
        assert MOE_TOPK == 2 and n_exp <= V7X_LANES
        wr = jnp.pad(w_router, ((0, 0), (0, V7X_LANES - n_exp)))
        in_specs.append(pl.BlockSpec((d, V7X_LANES), lambda i, k: (0, 0)))
        args.append(wr)
        out_shape += [jax.ShapeDtypeStruct((t, V7X_LANES), jnp.int32), jax.ShapeDtypeStruct((t, V7X_LANES), F32)]
        out_specs += [pl.BlockSpec((tm, V7X_LANES), lambda i, k: (i, 0))] * 2
    kern = functools.partial(_oproj_ln_kernel, alpha=alpha, n_exp=n_exp)
    return pl.pallas_call(
        kern,
        out_shape=out_shape,
        grid=(t // tm, kdim // tk),
        in_specs=in_specs,
        out_specs=out_specs,
        compiler_params=_params(("parallel", "arbitrary")),
        name="oproj_ln",
    )(*args)


def _route(top_e, tm, n_exp):
    t = top_e.shape[0]
    n_assign = t * MOE_TOPK
    a_exp = top_e.reshape(-1)
    onehot = (a_exp[:, None] == jnp.arange(n_exp, dtype=jnp.int32)[None, :]).astype(jnp.int32)
    csum = jnp.cumsum(onehot, axis=0)
    rank = jnp.sum(csum * onehot, axis=1) - 1
    counts = csum[-1]
    p_counts = (counts + tm - 1) // tm * tm
    p_ends = jnp.cumsum(p_counts)
    p_starts = p_ends - p_counts
    dest = (p_starts[a_exp] + rank).astype(jnp.int32)
    n_blocks = -(-n_assign // tm) + n_exp
    p_rows = n_blocks * tm
    a_tok = jnp.arange(n_assign, dtype=jnp.int32) // MOE_TOPK
    buf_tok = jnp.zeros((p_rows,), jnp.int32).at[dest].set(a_tok)
    blk_start = jnp.arange(n_blocks, dtype=jnp.int32) * tm
    blk_exp = jnp.minimum(jnp.searchsorted(p_ends, blk_start, side="right"), n_exp - 1).astype(jnp.int32)
    blk_rows = jnp.clip(p_starts[blk_exp] + counts[blk_exp] - blk_start, 0, tm).astype(jnp.int32)
    n_used = (p_ends[-1] // tm).astype(jnp.int32).reshape(1)
    return buf_tok, blk_exp, blk_rows, n_used, dest.reshape(t, MOE_TOPK)


def _row_copy(src_hbm, row, dst_vmem, r, sem):
    return pltpu.make_async_copy(src_hbm.at[pl.ds(row, 1)], dst_vmem.at[pl.ds(r, 1)], sem)


def _moe_kernel(exp_ref, rows_ref, used_ref, tok_ref, x_hbm, wg_ref, wu_ref, wd_ref, o_ref,
                xg_ref, xb_ref, wgb_ref, wub_ref, wdb_ref, sem, *, rc):
    i = pl.program_id(0)
    j = pl.program_id(1)
    tm = xg_ref.shape[0]
    n_used = used_ref[0]
    active = i < n_used

    def start_gather(tile):
        for c0 in range(0, tm, rc):
            @pl.when(c0 < rows_ref[tile])
            def _(c0=c0):
                def start(r, c):
                    _row_copy(x_hbm, tok_ref[tile * tm + c0 + r], xg_ref, c0 + r, sem).start()
                    return c

                lax.fori_loop(0, rc, start, 0, unroll=8)

    @pl.when(jnp.logical_and(i == 0, j == 0))
    def _():
        start_gather(0)

    @pl.when(jnp.logical_and(j == 1, i + 1 < n_used))
    def _():
        start_gather(i + 1)

    @pl.when(j == 0)
    def _():
        o_ref[...] = jnp.zeros_like(o_ref)

    @pl.when(jnp.logical_and(active, j == 0))
    def _():
        for c0 in range(0, tm, rc):
            @pl.when(c0 < rows_ref[i])
            def _(c0=c0):
                rows = pl.ds(c0, rc)
                pltpu.make_async_copy(x_hbm.at[rows], xg_ref.at[rows], sem).wait()
        for c0 in range(0, tm, rc):
            @pl.when(c0 < rows_ref[i])
            def _(c0=c0):
                rows = pl.ds(c0, rc)
                xb_ref[rows, :] = xg_ref[rows, :].astype(BF16)

    @pl.when(active)
    def _():
        _ffn_accumulate(xb_ref, wg_ref.at[0], wu_ref.at[0], wd_ref.at[0], wgb_ref, wub_ref, wdb_ref, o_ref,
                        _pick(tm, 512), n_rows=rows_ref[i], rc_part=rc)


def _moe(x2, buf_tok, blk_exp, blk_rows, n_used, w_gate, w_up, w_down, tm):
    t, d = x2.shape
    n_exp, _, f = w_gate.shape
    tf = _pick(f, 256)
    rc = _pick(tm, 256)
    n_blocks = blk_exp.shape[0]
    nj = f // tf
    assert nj >= 2

    def w_idx(i, j, exp_ref, rows_ref, used_ref, tok_ref):
        last = used_ref[0] - 1
        return exp_ref[jnp.minimum(i, last)], jnp.where(i <= last, j, nj - 1)

    def wgu_map(i, j, *s):
        e, jj = w_idx(i, j, *s)
        return e, 0, jj

    def wd_map(i, j, *s):
        e, jj = w_idx(i, j, *s)
        return e, jj, 0

    grid_spec = pltpu.PrefetchScalarGridSpec(
        num_scalar_prefetch=4,
        grid=(n_blocks, nj),
        in_specs=[
            pl.BlockSpec(memory_space=pl.ANY),
            pl.BlockSpec((1, d, tf), wgu_map),
            pl.BlockSpec((1, d, tf), wgu_map),
            pl.BlockSpec((1, tf, d), wd_map),
        ],
        out_specs=pl.BlockSpec((tm, d), lambda i, j, *s: (i, 0)),
        scratch_shapes=[pltpu.VMEM((tm, d), F32), pltpu.VMEM((tm, d), BF16), pltpu.VMEM((d, tf), BF16),
                        pltpu.VMEM((d, tf), BF16), pltpu.VMEM((tf, d), BF16), pltpu.SemaphoreType.DMA],
    )
    return pl.pallas_call(
        functools.partial(_moe_kernel, rc=rc),
        out_shape=jax.ShapeDtypeStruct((n_blocks * tm, d), F32),
        grid_spec=grid_spec,
        compiler_params=_params(("arbitrary", "arbitrary")),
        name="moe",
    )(blk_exp, blk_rows, n_used, buf_tok, x2, w_gate, w_up, w_down)


def _combine_ln_kernel(p0_ref, p1_ref, x_ref, gate_ref, y_hbm, g_ref, b_ref, o_ref, ybuf_ref, sems, *, alpha):
    i = pl.program_id(0)
    tm = x_ref.shape[0]
    slot = i % 2

    def start_gather(tile, sl):
        def start(r, c):
            for k, p_ref in enumerate((p0_ref, p1_ref)):
                _row_copy(y_hbm, p_ref[tile * tm + r], ybuf_ref.at[sl, k], r, sems.at[sl, k]).start()
            return c

        lax.fori_loop(0, tm, start, 0, unroll=8)

    @pl.when(i == 0)
    def _():
        start_gather(0, 0)

    @pl.when(i + 1 < pl.num_programs(0))
    def _():
        start_gather(i + 1, 1 - slot)

    for k in range(MOE_TOPK):
        pltpu.make_async_copy(y_hbm.at[pl.ds(0, tm)], ybuf_ref.at[slot, k], sems.at[slot, k]).wait()
    gates = gate_ref[...]
    y = ybuf_ref[slot, 0] * gates[:, 0:1] + ybuf_ref[slot, 1] * gates[:, 1:2]
    o_ref[...] = _layer_norm(alpha * x_ref[...] + y, g_ref[...], b_ref[...])


def _combine_ln(x2, gates, y_sorted, pos, g, b, alpha):
    t, d = x2.shape
    tm = _pick(t, 256)
    grid_spec = pltpu.PrefetchScalarGridSpec(
        num_scalar_prefetch=2,
        grid=(t // tm,),
        in_specs=[
            pl.BlockSpec((tm, d), lambda i, *s: (i, 0)),
            pl.BlockSpec((tm, gates.shape[1]), lambda i, *s: (i, 0)),
            pl.BlockSpec(memory_space=pl.ANY),
            pl.BlockSpec((1, d), lambda i, *s: (0, 0)),
            pl.BlockSpec((1, d), lambda i, *s: (0, 0)),
        ],
        out_specs=pl.BlockSpec((tm, d), lambda i, *s: (i, 0)),
        scratch_shapes=[pltpu.VMEM((2, MOE_TOPK, tm, d), F32), pltpu.SemaphoreType.DMA((2, MOE_TOPK))],
    )
    kern = functools.partial(_combine_ln_kernel, alpha=alpha)
    return pl.pallas_call(
        kern,
        out_shape=jax.ShapeDtypeStruct((t, d), F32),
        grid_spec=grid_spec,
        compiler_params=_params(("arbitrary",)),
        name="combine_ln",
    )(pos[:, 0], pos[:, 1], x2, gates, y_sorted, g.reshape(1, d), b.reshape(1, d))


def _rope_tables(s):
    inv = 1.0 / (ROPE_THETA ** (jnp.arange(0, HEAD_DIM, 2, dtype=F32) / HEAD_DIM))
    ang = jnp.arange(s, dtype=F32)[:, None] * inv[None, :]
    ang = jnp.concatenate([ang, ang], axis=-1)
    sign = jnp.where(jnp.arange(HEAD_DIM) < HEAD_DIM // 2, -1.0, 1.0).astype(F32)
    return jnp.cos(ang), jnp.sin(ang) * sign[None, :]


def kernel(x, pool_w, pool_scale, w_kv, moba_wq, moba_wo, ffn_w_gate, ffn_w_up, ffn_w_down,
           moe_router, moe_w_gate, moe_w_up, moe_w_down, ln_mix_g, ln_mix_b, ln_ffn_g, ln_ffn_b):
    bsz, s, d = x.shape
    t = bsz * s
    depth = ln_mix_g.shape[0]
    n_a = pool_w.shape[0]
    alpha = (2.0 * depth) ** 0.25
    cos, sin_signed = _rope_tables(s)
    nb = s // MOBA_BLOCK
    assert s % MOBA_BLOCK == 0 and d % HEAD_DIM == 0
    kv = None
    x2 = x.reshape(t, d)
    xb = None
    for l in range(depth):
        if l < n_a:
            x2 = _pool_ln(x2.reshape(bsz, s, d), pool_w[l], pool_scale[l], ln_mix_g[l], ln_mix_b[l],
                          alpha).reshape(t, d)
            router = None
        else:
            bl = l - n_a
            k, vt, k_mean = kv
            xb = _to_bf16(x2) if xb is None else xb
            q = _proj_rope(xb, _to_bf16(moba_wq[bl]), 0, d, cos, sin_signed, s, with_mean=False)[0]
            att = _moba(q.reshape(bsz, s, d), k.reshape(bsz, s, d), vt, k_mean)
            router = moe_router[l // 2] if l % 2 == 1 else None
            res = _oproj_ln(att.reshape(t, d), _to_bf16(moba_wo[bl]), x2, ln_mix_g[l], ln_mix_b[l], alpha, router)
            x2 = res[0]
        xb = None
        jf = l // 2
        if l % 2 == 0:
            x2, xb = _swiglu_ln(x2, ffn_w_gate[jf], ffn_w_up[jf], ffn_w_down[jf], ln_ffn_g[l], ln_ffn_b[l], alpha)
        else:
            n_exp = moe_router.shape[2]
            if router is None:
                raise NotImplementedError("MoE after a pooling mixer is not supported")
            tm = _pick(t * MOE_TOPK, 1024)
            buf_tok, blk_exp, blk_rows, n_used, pos = _route(res[1][:, :MOE_TOPK], tm, n_exp)
            y_sorted = _moe(x2, buf_tok, blk_exp, blk_rows, n_used,
                            moe_w_gate[jf], moe_w_up[jf], moe_w_down[jf], tm)
            x2 = _combine_ln(x2, res[2], y_sorted, pos, ln_ffn_g[l], ln_ffn_b[l], alpha)
        if l == n_a - 1:
            xb = _to_bf16(x2) if xb is None else xb
            wkvb = _to_bf16(w_kv)
            k, k_mean = _proj_rope(xb, wkvb, 0, d, cos, sin_signed, s, with_mean=True)
            vt = _proj_vt(xb, wkvb, d, d, bsz, s)
            kv = (k, vt, k_mean.reshape(bsz, nb, d))
    return x2.reshape(bsz, s, d)
```

```python
import functools

import jax
import jax.numpy as jnp
from jax import lax
from jax.experimental import pallas as pl
from jax.experimental.pallas import tpu as pltpu

HEAD_DIM = 128
MOBA_BLOCK = 256
MOBA_TOPK = 3
ROPE_THETA = 10000.0
POOL_WINDOWS = (2, 4, 8, 16)
MOE_TOPK = 2
LN_EPS = 1e-5
NEG = -1e30
LOG2_E = 1.4426950408889634

V7X_LANES = 128
V7X_SUBLANES = 8
V7X_VMEM_LIMIT_BYTES = 56 * 1024 * 1024

F32 = jnp.float32
BF16 = jnp.bfloat16


def _dot(a, b):
    return jnp.dot(a, b, preferred_element_type=F32)


def _dot_nt(a, b):
    return lax.dot_general(a, b, (((1,), (1,)), ((), ())), preferred_element_type=F32)


def _layer_norm(y, g, b):
    mu = jnp.mean(y, axis=-1, keepdims=True)
    d = y - mu
    var = jnp.mean(d * d, axis=-1, keepdims=True)
    return d * lax.rsqrt(var + LN_EPS) * g + b


def _params(semantics):
    return pltpu.CompilerParams(dimension_semantics=semantics, vmem_limit_bytes=V7X_VMEM_LIMIT_BYTES)


def _pick(dim, pref):
    t = min(dim, pref)
    while dim % t:
        t //= 2
    return t


def _ln_rows(x_ref, o_ref, g_ref, b_ref, alpha, rc):
    tm = o_ref.shape[0]
    for c0 in range(0, tm, rc):
        rows = pl.ds(c0, min(rc, tm - c0))
        o_ref[rows, :] = _layer_norm(alpha * x_ref[rows, :] + o_ref[rows, :], g_ref[...], b_ref[...])


def _pool_ln_kernel(xh_ref, x_ref, w_ref, sc_ref, g_ref, b_ref, o_ref, *, alpha, windows, halo):
    i = pl.program_id(1)
    ts, d = x_ref.shape[1], x_ref.shape[2]
    c = d // len(windows)
    t_pos = i * ts + lax.broadcasted_iota(jnp.int32, (ts, 1), 0)
    for g, w in enumerate(windows):
        cols = pl.ds(g * c, c)
        x = x_ref[0, :, cols]
        prev = jnp.where(i > 0, xh_ref[0, :, cols], 0.0)
        cur = jnp.concatenate([prev, x], axis=0)
        width = 1
        while width < w:
            cur = cur + pltpu.roll(cur, width, axis=0)
            width *= 2
        cnt = jnp.minimum(t_pos + 1, w).astype(F32)
        diff = cur[halo:, :] / cnt - x
        y = _dot(diff.astype(BF16), w_ref[g].astype(BF16))
        o_ref[0, :, cols] = y * sc_ref[:, cols]
    _ln_rows(x_ref.at[0], o_ref.at[0], g_ref, b_ref, alpha, 256)


def _pool_ln(x, w_groups, scale, g, b, alpha):
    bsz, s, d = x.shape
    windows = POOL_WINDOWS
    assert len(windows) == w_groups.shape[0] and d % len(windows) == 0
    assert all(w & (w - 1) == 0 for w in windows) and list(windows) == sorted(windows)
    halo = -(-max(windows) // V7X_SUBLANES) * V7X_SUBLANES
    ts = _pick(s, 512)
    assert ts % halo == 0
    r = ts // halo
    kern = functools.partial(_pool_ln_kernel, alpha=alpha, windows=windows, halo=halo)
    return pl.pallas_call(
        kern,
        out_shape=jax.ShapeDtypeStruct(x.shape, F32),
        grid=(bsz, s // ts),
        in_specs=[
            pl.BlockSpec((1, halo, d), lambda bi, i: (bi, jnp.maximum(i * r - 1, 0), 0)),
            pl.BlockSpec((1, ts, d), lambda bi, i: (bi, i, 0)),
            pl.BlockSpec(w_groups.shape, lambda bi, i: (0, 0, 0)),
            pl.BlockSpec((1, d), lambda bi, i: (0, 0)),
            pl.BlockSpec((1, d), lambda bi, i: (0, 0)),
            pl.BlockSpec((1, d), lambda bi, i: (0, 0)),
        ],
        out_specs=pl.BlockSpec((1, ts, d), lambda bi, i: (bi, i, 0)),
        compiler_params=_params(("parallel", "parallel")),
        name="pool_ln",
    )(x, x, w_groups, scale.reshape(1, d), g.reshape(1, d), b.reshape(1, d))


def _ffn_accumulate(xb_ref, wg_ref, wu_ref, wd_ref, wgb_ref, wub_ref, wdb_ref, o_ref, rc, n_rows=None, rc_part=None):
    wgb_ref[...] = wg_ref[...].astype(BF16)
    wub_ref[...] = wu_ref[...].astype(BF16)
    wdb_ref[...] = wd_ref[...].astype(BF16)
    tm = xb_ref.shape[0]

    def chunk(c0, size):
        rows = pl.ds(c0, size)
        xb = xb_ref[rows, :]
        a = _dot(xb, wgb_ref[...])
        u = _dot(xb, wub_ref[...])
        h = (a * jax.nn.sigmoid(a)) * u
        o_ref[rows, :] += _dot(h.astype(BF16), wdb_ref[...])

    def full():
        for c0 in range(0, tm, rc):
            chunk(c0, rc)

    if n_rows is None:
        full()
        return
    pl.when(n_rows == tm)(full)
    n_chunks = (n_rows + rc_part - 1) // rc_part
    for live in range(1, tm // rc_part + 1):
        @pl.when(jnp.logical_and(n_rows < tm, n_chunks == live))
        def _(live=live):
            for c in range(live):
                chunk(c * rc_part, rc_part)


def _swiglu_ln_kernel(x_ref, wg_ref, wu_ref, wd_ref, g_ref, b_ref, o_ref, ob_ref, wgb_ref, wub_ref, wdb_ref,
                      *, alpha):
    j = pl.program_id(1)

    @pl.when(j == 0)
    def _():
        ob_ref[...] = x_ref[...].astype(BF16)
        o_ref[...] = jnp.zeros_like(o_ref)

    _ffn_accumulate(ob_ref, wg_ref, wu_ref, wd_ref, wgb_ref, wub_ref, wdb_ref, o_ref, _pick(o_ref.shape[0], 512))

    @pl.when(j == pl.num_programs(1) - 1)
    def _():
        _ln_rows(x_ref, o_ref, g_ref, b_ref, alpha, 128)
        ob_ref[...] = o_ref[...].astype(BF16)


def _swiglu_ln(x2, w_gate, w_up, w_down, g, b, alpha):
    t, d = x2.shape
    f = w_gate.shape[1]
    tm = _pick(t, 1024)
    tf = _pick(f, 256)
    kern = functools.partial(_swiglu_ln_kernel, alpha=alpha)
    return pl.pallas_call(
        kern,
        out_shape=[jax.ShapeDtypeStruct((t, d), F32), jax.ShapeDtypeStruct((t, d), BF16)],
        grid=(t // tm, f // tf),
        in_specs=[
            pl.BlockSpec((tm, d), lambda i, j: (i, 0), pipeline_mode=pl.Buffered(1)),
            pl.BlockSpec((d, tf), lambda i, j: (0, j)),
            pl.BlockSpec((d, tf), lambda i, j: (0, j)),
            pl.BlockSpec((tf, d), lambda i, j: (j, 0)),
            pl.BlockSpec((1, d), lambda i, j: (0, 0)),
            pl.BlockSpec((1, d), lambda i, j: (0, 0)),
        ],
        out_specs=[pl.BlockSpec((tm, d), lambda i, j: (i, 0)), pl.BlockSpec((tm, d), lambda i, j: (i, 0))],
        scratch_shapes=[pltpu.VMEM((d, tf), BF16), pltpu.VMEM((d, tf), BF16), pltpu.VMEM((tf, d), BF16)],
        compiler_params=_params(("parallel", "arbitrary")),
        name="swiglu_ln",
    )(x2, w_gate, w_up, w_down, g.reshape(1, d), b.reshape(1, d))


def _to_bf16_kernel(w_ref, o_ref):
    o_ref[...] = w_ref[...].astype(BF16)


def _to_bf16(w):
    r, c = w.shape
    tr = _pick(r, 512)
    return pl.pallas_call(
        _to_bf16_kernel,
        out_shape=jax.ShapeDtypeStruct((r, c), BF16),
        grid=(r // tr,),
        in_specs=[pl.BlockSpec((tr, c), lambda i: (i, 0))],
        out_specs=pl.BlockSpec((tr, c), lambda i: (i, 0)),
        compiler_params=_params(("parallel",)),
        name="to_bf16",
    )(w)


def _proj_rope_kernel(x_ref, w_ref, cos_ref, sin_ref, o_ref, *km_ref, blk):
    tm, tn = o_ref.shape
    for c0 in range(0, tm, blk):
        rows = pl.ds(c0, blk)
        y = _dot(x_ref[rows, :], w_ref[...])
        cos = cos_ref[rows, :]
        sin = sin_ref[rows, :]
        heads = []
        for h in range(tn // HEAD_DIM):
            th = y[:, h * HEAD_DIM:(h + 1) * HEAD_DIM]
            heads.append(th * cos + pltpu.roll(th, HEAD_DIM // 2, axis=1) * sin)
        yr = heads[0] if len(heads) == 1 else jnp.concatenate(heads, axis=1)
        o_ref[rows, :] = yr.astype(BF16)
        if km_ref:
            km_ref[0][0, pl.ds(c0 // blk, 1), :] = jnp.sum(yr, axis=0, keepdims=True) * (1.0 / blk)


def _proj_rope(xb, wb, col0, n_out, cos, sin_signed, s, with_mean):
    t, d = xb.shape
    tm = _pick(s, 1024)
    tn = _pick(n_out, 1024)
    assert tm % MOBA_BLOCK == 0 and tn % HEAD_DIM == 0 and col0 % tn == 0
    s_tiles = s // tm
    jb = col0 // tn
    out_shape = [jax.ShapeDtypeStruct((t, n_out), BF16)]
    out_specs = [pl.BlockSpec((tm, tn), lambda i, j: (i, j))]
    if with_mean:
        out_shape.append(jax.ShapeDtypeStruct((t // tm, tm // MOBA_BLOCK, n_out), F32))
        out_specs.append(pl.BlockSpec((1, tm // MOBA_BLOCK, tn), lambda i, j: (i, 0, j)))
    kern = functools.partial(_proj_rope_kernel, blk=MOBA_BLOCK)
    return pl.pallas_call(
        kern,
        out_shape=out_shape,
        grid=(t // tm, n_out // tn),
        in_specs=[
            pl.BlockSpec((tm, d), lambda i, j: (i, 0)),
            pl.BlockSpec((d, tn), lambda i, j: (0, j + jb)),
            pl.BlockSpec((tm, HEAD_DIM), lambda i, j: (i % s_tiles, 0)),
            pl.BlockSpec((tm, HEAD_DIM), lambda i, j: (i % s_tiles, 0)),
        ],
        out_specs=out_specs,
        compiler_params=_params(("parallel", "arbitrary")),
        name="proj_rope_k" if with_mean else "proj_rope_q",
    )(xb, wb, cos, sin_signed)


def _proj_vt_kernel(x_ref, w_ref, o_ref, *, blk):
    tm = x_ref.shape[0]
    tn = w_ref.shape[1]
    for c in range(tm // blk):
        yt = _dot(x_ref[pl.ds(c * blk, blk), :], w_ref[...]).T
        for h in range(tn // HEAD_DIM):
            o_ref[0, h, c] = yt[h * HEAD_DIM:(h + 1) * HEAD_DIM, :].astype(BF16)


def _proj_vt(xb, wb, col0, n_out, bsz, s):
    t, d = xb.shape
    tm = _pick(s, 1024)
    tn = _pick(n_out, 1024)
    assert tm % MOBA_BLOCK == 0 and tn % HEAD_DIM == 0 and col0 % tn == 0
    s_tiles = s // tm
    jb = col0 // tn
    hpt = tn // HEAD_DIM
    nbt = tm // MOBA_BLOCK
    kern = functools.partial(_proj_vt_kernel, blk=MOBA_BLOCK)
    return pl.pallas_call(
        kern,
        out_shape=jax.ShapeDtypeStruct((bsz, n_out // HEAD_DIM, s // MOBA_BLOCK, HEAD_DIM, MOBA_BLOCK), BF16),
        grid=(t // tm, n_out // tn),
        in_specs=[
            pl.BlockSpec((tm, d), lambda i, j: (i, 0)),
            pl.BlockSpec((d, tn), lambda i, j: (0, j + jb)),
        ],
        out_specs=pl.BlockSpec((1, hpt, nbt, HEAD_DIM, MOBA_BLOCK),
                               lambda i, j: (i // s_tiles, j, i % s_tiles, 0, 0)),
        compiler_params=_params(("parallel", "arbitrary")),
        name="proj_vt",
    )(xb, wb)


def _moba_kernel(q_ref, k_ref, vt_ref, km_ref, o_ref, bias_ref, s_ref, m8_ref, m_ref, l8_ref, acc_ref,
                 *, scale, topk, heads, unroll):
    qi = pl.program_id(2)
    blk = q_ref.shape[1]
    nb = km_ref.shape[1]
    blk_id = lax.broadcasted_iota(jnp.int32, (nb, blk), 0)
    past = blk_id < qi
    kpos = lax.broadcasted_iota(jnp.int32, (blk, blk), 0)
    qpos = lax.broadcasted_iota(jnp.int32, (blk, blk), 1)
    own_rows = pl.ds(pl.multiple_of(qi * blk, blk), blk)

    def fold(x, op):
        return op(x.reshape(blk // V7X_SUBLANES, V7X_SUBLANES, blk), axis=0)

    for h in range(heads):
        hs = pl.ds(h * HEAD_DIM, HEAD_DIM)
        q = q_ref[0, :, hs]
        gate = _dot_nt(km_ref[0, :, hs].astype(BF16), q)
        gate = jnp.where(past, gate, NEG)
        rank = jnp.zeros((nb, blk), jnp.int32)
        for m in range(nb):
            gm = gate[m:m + 1, :]
            better = (gm > gate) | ((gm == gate) & (blk_id > m))
            rank = rank + better.astype(jnp.int32)
        bias_ref[h] = jnp.where(past & (rank < topk), 0.0, NEG)

        m8_ref[h] = jnp.full(m8_ref.shape[1:], NEG, F32)

    def scores(it, carry):
        for un in range(unroll):
            n = it * unroll + un
            nk = jnp.minimum(n, nb - 1)
            rows = pl.ds(pl.multiple_of(nk * blk, blk), blk)
            for h in range(heads):
                hs = pl.ds(h * HEAD_DIM, HEAD_DIM)
                s_n = _dot_nt(k_ref[0, rows, hs], q_ref[0, :, hs]) * scale + bias_ref[h, pl.ds(nk, 1), :]
                s_ref[h, n] = s_n
                m8_ref[h] = jnp.maximum(m8_ref[h], fold(s_n, jnp.max))
        return carry

    lax.fori_loop(0, (qi + unroll - 1) // unroll, scores, 0)

    for h in range(heads):
        hs = pl.ds(h * HEAD_DIM, HEAD_DIM)
        s = _dot_nt(k_ref[0, own_rows, hs], q_ref[0, :, hs]) * scale
        s = jnp.where(kpos <= qpos, s, NEG)
        s_ref[h, qi] = s
        for un in range(1, unroll):
            s_ref[h, qi + un] = jnp.full((blk, blk), NEG, F32)
        m_ref[h] = jnp.max(jnp.maximum(m8_ref[h], fold(s, jnp.max)), axis=0, keepdims=True)
        l8_ref[h] = jnp.zeros(l8_ref.shape[1:], F32)
        acc_ref[h] = jnp.zeros(acc_ref.shape[1:], F32)

    def values(it, carry):
        for un in range(unroll):
            n = it * unroll + un
            nv = jnp.minimum(n, nb - 1)
            for h in range(heads):
                p_n = jnp.exp2(s_ref[h, n] - m_ref[h])
                l8_ref[h] += fold(p_n, jnp.sum)
                acc_ref[h] += _dot(vt_ref[0, h, nv], p_n.astype(BF16))
        return carry

    lax.fori_loop(0, (qi + unroll) // unroll, values, 0)
    for h in range(heads):
        l = jnp.sum(l8_ref[h], axis=0, keepdims=True)
        o_ref[0, :, pl.ds(h * HEAD_DIM, HEAD_DIM)] = (acc_ref[h] / l).T.astype(BF16)


def _moba(q, k, vt, k_mean):
    bsz, s, d = q.shape
    n_heads = d // HEAD_DIM
    nb = s // MOBA_BLOCK
    heads = _pick(n_heads, 4)
    unroll = 2
    gw = heads * HEAD_DIM
    kern = functools.partial(_moba_kernel, scale=HEAD_DIM ** -0.5 * LOG2_E, topk=min(MOBA_TOPK, nb), heads=heads,
                             unroll=unroll)
    return pl.pallas_call(
        kern,
        out_shape=jax.ShapeDtypeStruct((bsz, s, d), BF16),
        grid=(bsz, n_heads // heads, nb),
        in_specs=[
            pl.BlockSpec((1, MOBA_BLOCK, gw), lambda b, h, i: (b, i, h)),
            pl.BlockSpec((1, s, gw), lambda b, h, i: (b, 0, h)),
            pl.BlockSpec((1, heads, nb, HEAD_DIM, MOBA_BLOCK), lambda b, h, i: (b, h, 0, 0, 0)),
            pl.BlockSpec((1, nb, gw), lambda b, h, i: (b, 0, h)),
        ],
        out_specs=pl.BlockSpec((1, MOBA_BLOCK, gw), lambda b, h, i: (b, i, h)),
        scratch_shapes=[pltpu.VMEM((heads, nb, MOBA_BLOCK), F32),
                        pltpu.VMEM((heads, nb + unroll - 1, MOBA_BLOCK, MOBA_BLOCK), F32),
                        pltpu.VMEM((heads, V7X_SUBLANES, MOBA_BLOCK), F32),
                        pltpu.VMEM((heads, 1, MOBA_BLOCK), F32),
                        pltpu.VMEM((heads, V7X_SUBLANES, MOBA_BLOCK), F32),
                        pltpu.VMEM((heads, HEAD_DIM, MOBA_BLOCK), F32)],
        compiler_params=_params(("parallel", "parallel", "arbitrary")),
        name="moba",
    )(q, k, vt, k_mean)


def _oproj_ln_kernel(a_ref, w_ref, x_hbm, g_ref, b_ref, *refs, alpha, n_exp):
    if n_exp:
        wr_ref, o_ref, idx_ref, gate_ref, x_ref, sem = refs
    else:
        o_ref, x_ref, sem = refs
    kk = pl.program_id(1)
    tm = o_ref.shape[0]
    rc = min(tm, 256)
    x_copy = pltpu.make_async_copy(x_hbm.at[pl.ds(pl.program_id(0) * tm, tm)], x_ref, sem)

    @pl.when(kk == 0)
    def _():
        x_copy.start()
        o_ref[...] = jnp.zeros_like(o_ref)

    for c0 in range(0, tm, rc):
        rows = pl.ds(c0, rc)
        o_ref[rows, :] += _dot(a_ref[rows, :], w_ref[...])

    @pl.when(kk == pl.num_programs(1) - 1)
    def _():
        x_copy.wait()
        for c0 in range(0, tm, rc):
            rows = pl.ds(c0, rc)
            y = _layer_norm(alpha * x_ref[rows, :] + o_ref[rows, :], g_ref[...], b_ref[...])
            o_ref[rows, :] = y
            if n_exp:
                logits = _dot(y.astype(BF16), wr_ref[...])
                lane = lax.broadcasted_iota(jnp.int32, logits.shape, 1)
                big = logits.shape[1]
                logits = jnp.where(lane < n_exp, logits, -jnp.inf)
                m1 = jnp.max(logits, axis=1, keepdims=True)
                i1 = jnp.min(jnp.where(logits == m1, lane, big), axis=1, keepdims=True)
                rest = jnp.where(lane == i1, -jnp.inf, logits)
                m2 = jnp.max(rest, axis=1, keepdims=True)
                i2 = jnp.min(jnp.where(rest == m2, lane, big), axis=1, keepdims=True)
                e2 = jnp.exp(m2 - m1)
                den = 1.0 + e2
                idx_ref[rows, :] = jnp.where(lane == 0, i1, jnp.where(lane == 1, i2, 0))
                gate_ref[rows, :] = jnp.where(lane == 0, 1.0 / den, jnp.where(lane == 1, e2 / den, 0.0))


def _oproj_ln(a2, w_o, x2, g, b, alpha, w_router=None):
    t, d = x2.shape
    kdim = a2.shape[1]
    tm = _pick(t, 1024)
    tk = _pick(kdim, 1024)
    n_exp = 0 if w_router is None else w_router.shape[1]
    in_specs = [
        pl.BlockSpec((tm, tk), lambda i, k: (i, k)),
        pl.BlockSpec((tk, d), lambda i, k: (k, 0)),
        pl.BlockSpec(memory_space=pl.ANY),
        pl.BlockSpec((1, d), lambda i, k: (0, 0)),
        pl.BlockSpec((1, d), lambda i, k: (0, 0)),
    ]
    args = [a2, w_o, x2, g.reshape(1, d), b.reshape(1, d)]
    out_shape = [jax.ShapeDtypeStruct((t, d), F32)]
    out_specs = [pl.BlockSpec((tm, d), lambda i, k: (i, 0))]
    if n_exp:
        assert MOE_TOPK == 2 and n_exp <= V7X_LANES
        wr = jnp.pad(w_router, ((0, 0), (0, V7X_LANES - n_exp))).astype(BF16)
        in_specs.append(pl.BlockSpec((d, V7X_LANES), lambda i, k: (0, 0)))
        args.append(wr)
        out_shape += [jax.ShapeDtypeStruct((t, V7X_LANES), jnp.int32), jax.ShapeDtypeStruct((t, V7X_LANES), F32)]
        out_specs += [pl.BlockSpec((tm, V7X_LANES), lambda i, k: (i, 0))] * 2
    kern = functools.partial(_oproj_ln_kernel, alpha=alpha, n_exp=n_exp)
    return pl.pallas_call(
        kern,
        out_shape=out_shape,
        grid=(t // tm, kdim // tk),
        in_specs=in_specs,
        out_specs=out_specs,
        scratch_shapes=[pltpu.VMEM((tm, d), F32), pltpu.SemaphoreType.DMA],
        compiler_params=_params(("parallel", "arbitrary")),
        name="oproj_ln",
    )(*args)


def _route(top_e, tm, n_exp):
    t = top_e.shape[0]
    n_assign = t * MOE_TOPK
    a_exp = top_e.reshape(-1)
    onehot = (a_exp[:, None] == jnp.arange(n_exp, dtype=jnp.int32)[None, :]).astype(jnp.int32)
    csum = jnp.cumsum(onehot, axis=0)
    rank = jnp.sum(csum * onehot, axis=1) - 1
    counts = csum[-1]
    p_counts = (counts + tm - 1) // tm * tm
    p_ends = jnp.cumsum(p_counts)
    p_starts = p_ends - p_counts
    dest = (p_starts[a_exp] + rank).astype(jnp.int32)
    n_blocks = -(-n_assign // tm) + n_exp
    p_rows = n_blocks * tm
    a_tok = jnp.arange(n_assign, dtype=jnp.int32) // MOE_TOPK
    buf_tok = jnp.zeros((p_rows,), jnp.int32).at[dest].set(a_tok)
    blk_start = jnp.arange(n_blocks, dtype=jnp.int32) * tm
    blk_exp = jnp.minimum(jnp.searchsorted(p_ends, blk_start, side="right"), n_exp - 1).astype(jnp.int32)
    blk_rows = jnp.clip(p_starts[blk_exp] + counts[blk_exp] - blk_start, 0, tm).astype(jnp.int32)
    n_used = (p_ends[-1] // tm).astype(jnp.int32).reshape(1)
    return buf_tok, blk_exp, blk_rows, n_used, dest.reshape(t, MOE_TOPK)


def _row_copy(src_hbm, row, dst_vmem, r, sem):
    return pltpu.make_async_copy(src_hbm.at[pl.ds(row, 1)], dst_vmem.at[pl.ds(r, 1)], sem)


def _moe_kernel(exp_ref, rows_ref, used_ref, tok_ref, x_hbm, wg_ref, wu_ref, wd_ref, o_ref,
                xg_ref, xb_ref, wgb_ref, wub_ref, wdb_ref, sem, *, rc):
    i = pl.program_id(0)
    j = pl.program_id(1)
    tm = xg_ref.shape[0]
    n_used = used_ref[0]
    active = i < n_used

    def start_gather(tile):
        for c0 in range(0, tm, rc):
            @pl.when(c0 < rows_ref[tile])
            def _(c0=c0):
                def start(r, c):
                    _row_copy(x_hbm, tok_ref[tile * tm + c0 + r], xg_ref, c0 + r, sem).start()
                    return c

                lax.fori_loop(0, rc, start, 0, unroll=8)

    @pl.when(jnp.logical_and(i == 0, j == 0))
    def _():
        start_gather(0)

    @pl.when(jnp.logical_and(j == 1, i + 1 < n_used))
    def _():
        start_gather(i + 1)

    @pl.when(j == 0)
    def _():
        o_ref[...] = jnp.zeros_like(o_ref)

    @pl.when(jnp.logical_and(active, j == 0))
    def _():
        for c0 in range(0, tm, rc):
            @pl.when(c0 < rows_ref[i])
            def _(c0=c0):
                rows = pl.ds(c0, rc)
                pltpu.make_async_copy(x_hbm.at[rows], xg_ref.at[rows], sem).wait()
        for c0 in range(0, tm, rc):
            @pl.when(c0 < rows_ref[i])
            def _(c0=c0):
                rows = pl.ds(c0, rc)
                xb_ref[rows, :] = xg_ref[rows, :].astype(BF16)

    @pl.when(active)
    def _():
        _ffn_accumulate(xb_ref, wg_ref.at[0], wu_ref.at[0], wd_ref.at[0], wgb_ref, wub_ref, wdb_ref, o_ref,
                        _pick(tm, 512), n_rows=rows_ref[i], rc_part=rc)


def _moe(x2, buf_tok, blk_exp, blk_rows, n_used, w_gate, w_up, w_down, tm):
    t, d = x2.shape
    n_exp, _, f = w_gate.shape
    tf = _pick(f, 256)
    rc = _pick(tm, 256)
    n_blocks = blk_exp.shape[0]
    nj = f // tf
    assert nj >= 2

    def w_idx(i, j, exp_ref, rows_ref, used_ref, tok_ref):
        last = used_ref[0] - 1
        return exp_ref[jnp.minimum(i, last)], jnp.where(i <= last, j, nj - 1)

    def wgu_map(i, j, *s):
        e, jj = w_idx(i, j, *s)
        return e, 0, jj

    def wd_map(i, j, *s):
        e, jj = w_idx(i, j, *s)
        return e, jj, 0

    grid_spec = pltpu.PrefetchScalarGridSpec(
        num_scalar_prefetch=4,
        grid=(n_blocks, nj),
        in_specs=[
            pl.BlockSpec(memory_space=pl.ANY),
            pl.BlockSpec((1, d, tf), wgu_map),
            pl.BlockSpec((1, d, tf), wgu_map),
            pl.BlockSpec((1, tf, d), wd_map),
        ],
        out_specs=pl.BlockSpec((tm, d), lambda i, j, *s: (i, 0)),
        scratch_shapes=[pltpu.VMEM((tm, d), F32), pltpu.VMEM((tm, d), BF16), pltpu.VMEM((d, tf), BF16),
                        pltpu.VMEM((d, tf), BF16), pltpu.VMEM((tf, d), BF16), pltpu.SemaphoreType.DMA],
    )
    return pl.pallas_call(
        functools.partial(_moe_kernel, rc=rc),
        out_shape=jax.ShapeDtypeStruct((n_blocks * tm, d), F32),
        grid_spec=grid_spec,
        compiler_params=_params(("arbitrary", "arbitrary")),
        name="moe",
    )(blk_exp, blk_rows, n_used, buf_tok, x2, w_gate, w_up, w_down)


def _combine_ln_kernel(p0_ref, p1_ref, x_ref, gate_ref, y_hbm, g_ref, b_ref, o_ref, ybuf_ref, sems, *, alpha):
    i = pl.program_id(0)
    tm = x_ref.shape[0]
    slot = i % 2

    def start_gather(tile, sl):
        def start(r, c):
            for k, p_ref in enumerate((p0_ref, p1_ref)):
                _row_copy(y_hbm, p_ref[tile * tm + r], ybuf_ref.at[sl, k], r, sems.at[sl, k]).start()
            return c

        lax.fori_loop(0, tm, start, 0, unroll=8)

    @pl.when(i == 0)
    def _():
        start_gather(0, 0)

    @pl.when(i + 1 < pl.num_programs(0))
    def _():
        start_gather(i + 1, 1 - slot)

    for k in range(MOE_TOPK):
        pltpu.make_async_copy(y_hbm.at[pl.ds(0, tm)], ybuf_ref.at[slot, k], sems.at[slot, k]).wait()
    gates = gate_ref[...]
    y = ybuf_ref[slot, 0] * gates[:, 0:1] + ybuf_ref[slot, 1] * gates[:, 1:2]
    o_ref[...] = _layer_norm(alpha * x_ref[...] + y, g_ref[...], b_ref[...])


def _combine_ln(x2, gates, y_sorted, pos, g, b, alpha):
    t, d = x2.shape
    tm = _pick(t, 256)
    grid_spec = pltpu.PrefetchScalarGridSpec(
        num_scalar_prefetch=2,
        grid=(t // tm,),
        in_specs=[
            pl.BlockSpec((tm, d), lambda i, *s: (i, 0)),
            pl.BlockSpec((tm, gates.shape[1]), lambda i, *s: (i, 0)),
            pl.BlockSpec(memory_space=pl.ANY),
            pl.BlockSpec((1, d), lambda i, *s: (0, 0)),
            pl.BlockSpec((1, d), lambda i, *s: (0, 0)),
        ],
        out_specs=pl.BlockSpec((tm, d), lambda i, *s: (i, 0)),
        scratch_shapes=[pltpu.VMEM((2, MOE_TOPK, tm, d), F32), pltpu.SemaphoreType.DMA((2, MOE_TOPK))],
    )
    kern = functools.partial(_combine_ln_kernel, alpha=alpha)
    return pl.pallas_call(
        kern,
        out_shape=jax.ShapeDtypeStruct((t, d), F32),
        grid_spec=grid_spec,
        compiler_params=_params(("arbitrary",)),
        name="combine_ln",
    )(pos[:, 0], pos[:, 1], x2, gates, y_sorted, g.reshape(1, d), b.reshape(1, d))


def _rope_tables(s):
    inv = 1.0 / (ROPE_THETA ** (jnp.arange(0, HEAD_DIM, 2, dtype=F32) / HEAD_DIM))
    ang = jnp.arange(s, dtype=F32)[:, None] * inv[None, :]
    ang = jnp.concatenate([ang, ang], axis=-1)
    sign = jnp.where(jnp.arange(HEAD_DIM) < HEAD_DIM // 2, -1.0, 1.0).astype(F32)
    return jnp.cos(ang), jnp.sin(ang) * sign[None, :]


def kernel(x, pool_w, pool_scale, w_kv, moba_wq, moba_wo, ffn_w_gate, ffn_w_up, ffn_w_down,
           moe_router, moe_w_gate, moe_w_up, moe_w_down, ln_mix_g, ln_mix_b, ln_ffn_g, ln_ffn_b):
    bsz, s, d = x.shape
    t = bsz * s
    depth = ln_mix_g.shape[0]
    n_a = pool_w.shape[0]
    alpha = (2.0 * depth) ** 0.25
    cos, sin_signed = _rope_tables(s)
    nb = s // MOBA_BLOCK
    assert s % MOBA_BLOCK == 0 and d % HEAD_DIM == 0
    kv = None
    x2 = x.reshape(t, d)
    xb = None
    for l in range(depth):
        if l < n_a:
            x2 = _pool_ln(x2.reshape(bsz, s, d), pool_w[l], pool_scale[l], ln_mix_g[l], ln_mix_b[l],
                          alpha).reshape(t, d)
            router = None
        else:
            bl = l - n_a
            k, vt, k_mean = kv
            xb = _to_bf16(x2) if xb is None else xb
            q = _proj_rope(xb, _to_bf16(moba_wq[bl]), 0, d, cos, sin_signed, s, with_mean=False)[0]
            att = _moba(q.reshape(bsz, s, d), k.reshape(bsz, s, d), vt, k_mean)
            router = moe_router[l // 2] if l % 2 == 1 else None
            res = _oproj_ln(att.reshape(t, d), _to_bf16(moba_wo[bl]), x2, ln_mix_g[l], ln_mix_b[l], alpha, router)
            x2 = res[0]
        xb = None
        jf = l // 2
        if l % 2 == 0:
            x2, xb = _swiglu_ln(x2, ffn_w_gate[jf], ffn_w_up[jf], ffn_w_down[jf], ln_ffn_g[l], ln_ffn_b[l], alpha)
        else:
            n_exp = moe_router.shape[2]
            if router is None:
                raise NotImplementedError("MoE after a pooling mixer is not supported")
            tm = _pick(t * MOE_TOPK, 1024)
            buf_tok, blk_exp, blk_rows, n_used, pos = _route(res[1][:, :MOE_TOPK], tm, n_exp)
            y_sorted = _moe(x2, buf_tok, blk_exp, blk_rows, n_used,
                            moe_w_gate[jf], moe_w_up[jf], moe_w_down[jf], tm)
            x2 = _combine_ln(x2, res[2], y_sorted, pos, ln_ffn_g[l], ln_ffn_b[l], alpha)
        if l == n_a - 1:
            xb = _to_bf16(x2) if xb is None else xb
            wkvb = _to_bf16(w_kv)
            k, k_mean = _proj_rope(xb, wkvb, 0, d, cos, sin_signed, s, with_mean=True)
            vt = _proj_vt(xb, wkvb, d, d, bsz, s)
            kv = (k, vt, k_mean.reshape(bsz, nb, d))
    return x2.reshape(bsz, s, d)
```

```python
import functools

import jax
import jax.numpy as jnp
from jax import lax
from jax.experimental import pallas as pl
from jax.experimental.pallas import tpu as pltpu

HEAD_DIM = 128
MOBA_BLOCK = 256
MOBA_TOPK = 3
ROPE_THETA = 10000.0
POOL_WINDOWS = (2, 4, 8, 16)
MOE_TOPK = 2
LN_EPS = 1e-5
NEG = -1e30
LOG2_E = 1.4426950408889634
MOBA_Q_SCALE = HEAD_DIM ** -0.5 * LOG2_E

V7X_LANES = 128
V7X_SUBLANES = 8
V7X_VMEM_LIMIT_BYTES = 56 * 1024 * 1024

F32 = jnp.float32
BF16 = jnp.bfloat16


def _dot(a, b):
    return jnp.dot(a, b, preferred_element_type=F32)


def _dot_nt(a, b):
    return lax.dot_general(a, b, (((1,), (1,)), ((), ())), preferred_element_type=F32)


def _layer_norm(y, g, b):
    mu = jnp.mean(y, axis=-1, keepdims=True)
    d = y - mu
    var = jnp.mean(d * d, axis=-1, keepdims=True)
    return d * lax.rsqrt(var + LN_EPS) * g + b


def _params(semantics):
    return pltpu.CompilerParams(dimension_semantics=semantics, vmem_limit_bytes=V7X_VMEM_LIMIT_BYTES)


def _pick(dim, pref):
    t = min(dim, pref)
    while dim % t:
        t //= 2
    return t


def _ln_rows(x_ref, o_ref, g_ref, b_ref, alpha, rc):
    tm = o_ref.shape[0]
    for c0 in range(0, tm, rc):
        rows = pl.ds(c0, min(rc, tm - c0))
        o_ref[rows, :] = _layer_norm(alpha * x_ref[rows, :] + o_ref[rows, :], g_ref[...], b_ref[...])


def _pool_ln_kernel(xh_ref, x_ref, w_ref, sc_ref, g_ref, b_ref, o_ref, ob_ref, *, alpha, windows, halo):
    i = pl.program_id(1)
    ts, d = x_ref.shape[1], x_ref.shape[2]
    c = d // len(windows)
    t_pos = i * ts + lax.broadcasted_iota(jnp.int32, (ts, 1), 0)
    for g, w in enumerate(windows):
        cols = pl.ds(g * c, c)
        x = x_ref[0, :, cols]
        prev = jnp.where(i > 0, xh_ref[0, :, cols], 0.0)
        cur = jnp.concatenate([prev, x], axis=0)
        width = 1
        while width < w:
            cur = cur + pltpu.roll(cur, width, axis=0)
            width *= 2
        cnt = jnp.minimum(t_pos + 1, w).astype(F32)
        diff = cur[halo:, :] / cnt - x
        y = _dot(diff.astype(BF16), w_ref[g].astype(BF16))
        o_ref[0, :, cols] = y * sc_ref[:, cols]
    _ln_rows(x_ref.at[0], o_ref.at[0], g_ref, b_ref, alpha, 256)
    ob_ref[0] = o_ref[0].astype(BF16)


def _pool_ln(x, w_groups, scale, g, b, alpha):
    bsz, s, d = x.shape
    windows = POOL_WINDOWS
    assert len(windows) == w_groups.shape[0] and d % len(windows) == 0
    assert all(w & (w - 1) == 0 for w in windows) and list(windows) == sorted(windows)
    halo = -(-max(windows) // V7X_SUBLANES) * V7X_SUBLANES
    ts = _pick(s, 512)
    assert ts % halo == 0
    r = ts // halo
    kern = functools.partial(_pool_ln_kernel, alpha=alpha, windows=windows, halo=halo)
    return pl.pallas_call(
        kern,
        out_shape=[jax.ShapeDtypeStruct(x.shape, F32), jax.ShapeDtypeStruct(x.shape, BF16)],
        grid=(bsz, s // ts),
        in_specs=[
            pl.BlockSpec((1, halo, d), lambda bi, i: (bi, jnp.maximum(i * r - 1, 0), 0)),
            pl.BlockSpec((1, ts, d), lambda bi, i: (bi, i, 0)),
            pl.BlockSpec(w_groups.shape, lambda bi, i: (0, 0, 0)),
            pl.BlockSpec((1, d), lambda bi, i: (0, 0)),
            pl.BlockSpec((1, d), lambda bi, i: (0, 0)),
            pl.BlockSpec((1, d), lambda bi, i: (0, 0)),
        ],
        out_specs=[pl.BlockSpec((1, ts, d), lambda bi, i: (bi, i, 0))] * 2,
        compiler_params=_params(("parallel", "parallel")),
        name="pool_ln",
    )(x, x, w_groups, scale.reshape(1, d), g.reshape(1, d), b.reshape(1, d))


def _ffn_accumulate(xb_ref, wg_ref, wu_ref, wd_ref, wgb_ref, wub_ref, wdb_ref, o_ref, rc, n_rows=None, rc_part=None):
    wgb_ref[...] = wg_ref[...].astype(BF16)
    wub_ref[...] = wu_ref[...].astype(BF16)
    wdb_ref[...] = wd_ref[...].astype(BF16)
    tm = xb_ref.shape[0]

    def chunk(c0, size):
        rows = pl.ds(c0, size)
        xb = xb_ref[rows, :]
        a = _dot(xb, wgb_ref[...])
        u = _dot(xb, wub_ref[...])
        h = (a * jax.nn.sigmoid(a)) * u
        o_ref[rows, :] += _dot(h.astype(BF16), wdb_ref[...])

    def full():
        for c0 in range(0, tm, rc):
            chunk(c0, rc)

    if n_rows is None:
        full()
        return
    pl.when(n_rows == tm)(full)
    n_chunks = (n_rows + rc_part - 1) // rc_part
    for live in range(1, tm // rc_part + 1):
        @pl.when(jnp.logical_and(n_rows < tm, n_chunks == live))
        def _(live=live):
            for c in range(live):
                chunk(c * rc_part, rc_part)


def _swiglu_ln_kernel(xb_ref, x_hbm, wg_ref, wu_ref, wd_ref, g_ref, b_ref, o_ref, ob_ref, wgb_ref, wub_ref, wdb_ref,
                      xr_ref, sems, *, alpha):
    i = pl.program_id(0)
    j = pl.program_id(1)
    tm = o_ref.shape[0]
    rl = xr_ref.shape[1]

    def x_copy(c):
        return pltpu.make_async_copy(x_hbm.at[pl.ds(i * tm + c * rl, rl)], xr_ref.at[c % 2], sems.at[c % 2])

    @pl.when(j == 0)
    def _():
        x_copy(0).start()
        o_ref[...] = jnp.zeros_like(o_ref)

    _ffn_accumulate(xb_ref, wg_ref, wu_ref, wd_ref, wgb_ref, wub_ref, wdb_ref, o_ref, _pick(tm, 512))

    @pl.when(j == pl.num_programs(1) - 1)
    def _():
        n_chunks = tm // rl
        for c in range(n_chunks):
            if c + 1 < n_chunks:
                x_copy(c + 1).start()
            x_copy(c).wait()
            rows = pl.ds(c * rl, rl)
            y = _layer_norm(alpha * xr_ref[c % 2] + o_ref[rows, :], g_ref[...], b_ref[...])
            o_ref[rows, :] = y
            ob_ref[rows, :] = y.astype(BF16)


def _swiglu_ln(x2, xb, w_gate, w_up, w_down, g, b, alpha):
    t, d = x2.shape
    f = w_gate.shape[1]
    tm = _pick(t, 1024)
    tf = _pick(f, 256)
    rl = _pick(tm, 128)
    kern = functools.partial(_swiglu_ln_kernel, alpha=alpha)
    return pl.pallas_call(
        kern,
        out_shape=[jax.ShapeDtypeStruct((t, d), F32), jax.ShapeDtypeStruct((t, d), BF16)],
        grid=(t // tm, f // tf),
        in_specs=[
            pl.BlockSpec((tm, d), lambda i, j: (i, 0)),
            pl.BlockSpec(memory_space=pl.ANY),
            pl.BlockSpec((d, tf), lambda i, j: (0, j)),
            pl.BlockSpec((d, tf), lambda i, j: (0, j)),
            pl.BlockSpec((tf, d), lambda i, j: (j, 0)),
            pl.BlockSpec((1, d), lambda i, j: (0, 0)),
            pl.BlockSpec((1, d), lambda i, j: (0, 0)),
        ],
        out_specs=[pl.BlockSpec((tm, d), lambda i, j: (i, 0)), pl.BlockSpec((tm, d), lambda i, j: (i, 0))],
        scratch_shapes=[pltpu.VMEM((d, tf), BF16), pltpu.VMEM((d, tf), BF16), pltpu.VMEM((tf, d), BF16),
                        pltpu.VMEM((2, rl, d), F32), pltpu.SemaphoreType.DMA((2,))],
        compiler_params=_params(("parallel", "arbitrary")),
        name="swiglu_ln",
    )(xb, x2, w_gate, w_up, w_down, g.reshape(1, d), b.reshape(1, d))


def _to_bf16_kernel(w_ref, o_ref):
    o_ref[...] = w_ref[...].astype(BF16)


def _to_bf16(w):
    r, c = w.shape
    tr = _pick(r, 512)
    return pl.pallas_call(
        _to_bf16_kernel,
        out_shape=jax.ShapeDtypeStruct((r, c), BF16),
        grid=(r // tr,),
        in_specs=[pl.BlockSpec((tr, c), lambda i: (i, 0))],
        out_specs=pl.BlockSpec((tr, c), lambda i: (i, 0)),
        compiler_params=_params(("parallel",)),
        name="to_bf16",
    )(w)


def _proj_rope_kernel(x_ref, w_ref, cos_ref, sin_ref, o_ref, *km_ref, blk, out_scale):
    tm, tn = o_ref.shape
    for c0 in range(0, tm, blk):
        rows = pl.ds(c0, blk)
        y = _dot(x_ref[rows, :], w_ref[...])
        cos = cos_ref[rows, :]
        sin = sin_ref[rows, :]
        if out_scale != 1.0:
            cos, sin = cos * out_scale, sin * out_scale
        heads = []
        for h in range(tn // HEAD_DIM):
            th = y[:, h * HEAD_DIM:(h + 1) * HEAD_DIM]
            heads.append(th * cos + pltpu.roll(th, HEAD_DIM // 2, axis=1) * sin)
        yr = heads[0] if len(heads) == 1 else jnp.concatenate(heads, axis=1)
        o_ref[rows, :] = yr.astype(BF16)
        if km_ref:
            km_ref[0][0, pl.ds(c0 // blk, 1), :] = jnp.sum(yr, axis=0, keepdims=True) * (1.0 / blk)


def _proj_rope(xb, wb, col0, n_out, cos, sin_signed, s, with_mean, out_scale=1.0):
    t, d = xb.shape
    tm = _pick(s, 1024)
    tn = _pick(n_out, 1024)
    assert tm % MOBA_BLOCK == 0 and tn % HEAD_DIM == 0 and col0 % tn == 0
    s_tiles = s // tm
    jb = col0 // tn
    out_shape = [jax.ShapeDtypeStruct((t, n_out), BF16)]
    out_specs = [pl.BlockSpec((tm, tn), lambda i, j: (i, j))]
    if with_mean:
        out_shape.append(jax.ShapeDtypeStruct((t // tm, tm // MOBA_BLOCK, n_out), F32))
        out_specs.append(pl.BlockSpec((1, tm // MOBA_BLOCK, tn), lambda i, j: (i, 0, j)))
    assert not (with_mean and out_scale != 1.0)
    kern = functools.partial(_proj_rope_kernel, blk=MOBA_BLOCK, out_scale=out_scale)
    return pl.pallas_call(
        kern,
        out_shape=out_shape,
        grid=(t // tm, n_out // tn),
        in_specs=[
            pl.BlockSpec((tm, d), lambda i, j: (i, 0)),
            pl.BlockSpec((d, tn), lambda i, j: (0, j + jb)),
            pl.BlockSpec((tm, HEAD_DIM), lambda i, j: (i % s_tiles, 0)),
            pl.BlockSpec((tm, HEAD_DIM), lambda i, j: (i % s_tiles, 0)),
        ],
        out_specs=out_specs,
        compiler_params=_params(("parallel", "arbitrary")),
        name="proj_rope_k" if with_mean else "proj_rope_q",
    )(xb, wb, cos, sin_signed)


def _proj_vt_kernel(x_ref, w_ref, o_ref, *, blk):
    tm = x_ref.shape[0]
    tn = w_ref.shape[1]
    for c in range(tm // blk):
        yt = _dot(x_ref[pl.ds(c * blk, blk), :], w_ref[...]).T
        for h in range(tn // HEAD_DIM):
            o_ref[0, h, c] = yt[h * HEAD_DIM:(h + 1) * HEAD_DIM, :].astype(BF16)


def _proj_vt(xb, wb, col0, n_out, bsz, s):
    t, d = xb.shape
    tm = _pick(s, 1024)
    tn = _pick(n_out, 1024)
    assert tm % MOBA_BLOCK == 0 and tn % HEAD_DIM == 0 and col0 % tn == 0
    s_tiles = s // tm
    jb = col0 // tn
    hpt = tn // HEAD_DIM
    nbt = tm // MOBA_BLOCK
    kern = functools.partial(_proj_vt_kernel, blk=MOBA_BLOCK)
    return pl.pallas_call(
        kern,
        out_shape=jax.ShapeDtypeStruct((bsz, n_out // HEAD_DIM, s // MOBA_BLOCK, HEAD_DIM, MOBA_BLOCK), BF16),
        grid=(t // tm, n_out // tn),
        in_specs=[
            pl.BlockSpec((tm, d), lambda i, j: (i, 0)),
            pl.BlockSpec((d, tn), lambda i, j: (0, j + jb)),
        ],
        out_specs=pl.BlockSpec((1, hpt, nbt, HEAD_DIM, MOBA_BLOCK),
                               lambda i, j: (i // s_tiles, j, i % s_tiles, 0, 0)),
        compiler_params=_params(("parallel", "arbitrary")),
        name="proj_vt",
    )(xb, wb)


def _moba_kernel(q_ref, k_ref, vt_ref, km_ref, o_ref, bias_ref, sa_ref, sb_ref, ma_ref, mb_ref, m8_ref, l8_ref,
                 acc_ref, **kw):
    even = pl.program_id(2) % 2 == 0
    args = (q_ref, k_ref, vt_ref, km_ref, o_ref, bias_ref, m8_ref, l8_ref, acc_ref)
    pl.when(even)(functools.partial(_moba_step, *args, sa_ref, ma_ref, sb_ref, mb_ref, **kw))
    pl.when(jnp.logical_not(even))(functools.partial(_moba_step, *args, sb_ref, mb_ref, sa_ref, ma_ref, **kw))


def _moba_step(q_ref, k_ref, vt_ref, km_ref, o_ref, bias_ref, m8_ref, l8_ref, acc_ref, s_cur, m_cur, s_prev, m_prev,
               *, topk, heads, unroll):
    i = pl.program_id(2)
    blk = q_ref.shape[1]
    nb = km_ref.shape[1]
    trips = (i + unroll - 1) // unroll
    ones = jnp.ones((V7X_SUBLANES, blk), BF16)

    def fold(x, op):
        return op(x.reshape(blk // V7X_SUBLANES, V7X_SUBLANES, blk), axis=0)

    def score_blocks(it):
        for un in range(unroll):
            n = it * unroll + un
            nk = jnp.minimum(n, nb - 1)
            rows = pl.ds(pl.multiple_of(nk * blk, blk), blk)
            for h in range(heads):
                hs = pl.ds(h * HEAD_DIM, HEAD_DIM)
                s_n = _dot_nt(k_ref[0, rows, hs], q_ref[0, :, hs]) + bias_ref[h, pl.ds(nk, 1), :]
                s_cur[h, n] = s_n
                m8_ref[h] = jnp.maximum(m8_ref[h], fold(s_n, jnp.max))

    def value_blocks(it):
        for un in range(unroll):
            n = it * unroll + un
            nv = jnp.minimum(n, nb - 1)
            for h in range(heads):
                p_n = jnp.exp2(s_prev[h, n] - m_prev[h]).astype(BF16)
                l8_ref[h] += _dot(ones, p_n)
                acc_ref[h] += _dot(vt_ref[0, h, nv], p_n)

    @pl.when(i < nb)
    def _():
        blk_id = lax.broadcasted_iota(jnp.int32, (nb, blk), 0)
        past = blk_id < i
        for h in range(heads):
            hs = pl.ds(h * HEAD_DIM, HEAD_DIM)
            gate = _dot_nt(km_ref[0, :, hs].astype(BF16), q_ref[0, :, hs])
            gate = jnp.where(past, gate, NEG)
            rank = jnp.zeros((nb, blk), jnp.int32)
            for m in range(nb):
                gm = gate[m:m + 1, :]
                better = (gm > gate) | ((gm == gate) & (blk_id > m))
                rank = rank + better.astype(jnp.int32)
            bias_ref[h] = jnp.where(past & (rank < topk), 0.0, NEG)
            m8_ref[h] = jnp.full(m8_ref.shape[1:], NEG, F32)

    for h in range(heads):
        l8_ref[h] = jnp.zeros(l8_ref.shape[1:], F32)
        acc_ref[h] = jnp.zeros(acc_ref.shape[1:], F32)

    @pl.when(i < nb)
    def _():
        def both(it, carry):
            score_blocks(it)
            value_blocks(it)
            return carry

        lax.fori_loop(0, trips, both, 0)

    @pl.when(i == nb)
    def _():
        def only_values(it, carry):
            value_blocks(it)
            return carry

        lax.fori_loop(0, trips, only_values, 0)

    @pl.when(i < nb)
    def _():
        kpos = lax.broadcasted_iota(jnp.int32, (blk, blk), 0)
        qpos = lax.broadcasted_iota(jnp.int32, (blk, blk), 1)
        own_rows = pl.ds(pl.multiple_of(i * blk, blk), blk)
        for h in range(heads):
            hs = pl.ds(h * HEAD_DIM, HEAD_DIM)
            s = _dot_nt(k_ref[0, own_rows, hs], q_ref[0, :, hs])
            s = jnp.where(kpos <= qpos, s, NEG)
            s_cur[h, i] = s
            for un in range(1, unroll):
                s_cur[h, i + un] = jnp.full((blk, blk), NEG, F32)
            m_cur[h] = jnp.max(jnp.maximum(m8_ref[h], fold(s, jnp.max)), axis=0, keepdims=True)

    @pl.when(i > 0)
    def _():
        for h in range(heads):
            o_ref[0, :, pl.ds(h * HEAD_DIM, HEAD_DIM)] = (acc_ref[h] / l8_ref[h, 0:1, :]).T.astype(BF16)


def _moba(q, k, vt, k_mean):
    bsz, s, d = q.shape
    n_heads = d // HEAD_DIM
    nb = s // MOBA_BLOCK
    heads = _pick(n_heads, 4)
    unroll = 2
    gw = heads * HEAD_DIM
    kern = functools.partial(_moba_kernel, topk=min(MOBA_TOPK, nb), heads=heads, unroll=unroll)
    return pl.pallas_call(
        kern,
        out_shape=jax.ShapeDtypeStruct((bsz, s, d), BF16),
        grid=(bsz, n_heads // heads, nb + 1),
        in_specs=[
            pl.BlockSpec((1, MOBA_BLOCK, gw), lambda b, h, i: (b, jnp.minimum(i, nb - 1), h)),
            pl.BlockSpec((1, s, gw), lambda b, h, i: (b, 0, h)),
            pl.BlockSpec((1, heads, nb, HEAD_DIM, MOBA_BLOCK), lambda b, h, i: (b, h, 0, 0, 0)),
            pl.BlockSpec((1, nb, gw), lambda b, h, i: (b, 0, h)),
        ],
        out_specs=pl.BlockSpec((1, MOBA_BLOCK, gw), lambda b, h, i: (b, jnp.maximum(i - 1, 0), h)),
        scratch_shapes=[pltpu.VMEM((heads, nb, MOBA_BLOCK), F32),
                        pltpu.VMEM((heads, nb + unroll - 1, MOBA_BLOCK, MOBA_BLOCK), F32),
                        pltpu.VMEM((heads, nb + unroll - 1, MOBA_BLOCK, MOBA_BLOCK), F32),
                        pltpu.VMEM((heads, 1, MOBA_BLOCK), F32),
                        pltpu.VMEM((heads, 1, MOBA_BLOCK), F32),
                        pltpu.VMEM((heads, V7X_SUBLANES, MOBA_BLOCK), F32),
                        pltpu.VMEM((heads, V7X_SUBLANES, MOBA_BLOCK), F32),
                        pltpu.VMEM((heads, HEAD_DIM, MOBA_BLOCK), F32)],
        compiler_params=_params(("parallel", "parallel", "arbitrary")),
        name="moba",
    )(q, k, vt, k_mean)


def _oproj_ln_kernel(a_ref, w_ref, x_hbm, g_ref, b_ref, *refs, alpha, n_exp):
    if n_exp:
        wr_ref, o_ref, idx_ref, gate_ref, x_ref, sem = refs
    else:
        o_ref, x_ref, sem = refs
    kk = pl.program_id(1)
    tm = o_ref.shape[0]
    rc = min(tm, 256)
    x_copy = pltpu.make_async_copy(x_hbm.at[pl.ds(pl.program_id(0) * tm, tm)], x_ref, sem)

    @pl.when(kk == 0)
    def _():
        x_copy.start()
        o_ref[...] = jnp.zeros_like(o_ref)

    for c0 in range(0, tm, rc):
        rows = pl.ds(c0, rc)
        o_ref[rows, :] += _dot(a_ref[rows, :], w_ref[...])

    @pl.when(kk == pl.num_programs(1) - 1)
    def _():
        x_copy.wait()
        for c0 in range(0, tm, rc):
            rows = pl.ds(c0, rc)
            y = _layer_norm(alpha * x_ref[rows, :] + o_ref[rows, :], g_ref[...], b_ref[...])
            o_ref[rows, :] = y
            if n_exp:
                logits = _dot(y.astype(BF16), wr_ref[...])
                lane = lax.broadcasted_iota(jnp.int32, logits.shape, 1)
                big = logits.shape[1]
                logits = jnp.where(lane < n_exp, logits, -jnp.inf)
                m1 = jnp.max(logits, axis=1, keepdims=True)
                i1 = jnp.min(jnp.where(logits == m1, lane, big), axis=1, keepdims=True)
                rest = jnp.where(lane == i1, -jnp.inf, logits)
                m2 = jnp.max(rest, axis=1, keepdims=True)
                i2 = jnp.min(jnp.where(rest == m2, lane, big), axis=1, keepdims=True)
                e2 = jnp.exp(m2 - m1)
                den = 1.0 + e2
                idx_ref[rows, :] = jnp.where(lane == 0, i1, jnp.where(lane == 1, i2, 0))
                gate_ref[rows, :] = jnp.where(lane == 0, 1.0 / den, jnp.where(lane == 1, e2 / den, 0.0))


def _oproj_ln(a2, w_o, x2, g, b, alpha, w_router=None):
    t, d = x2.shape
    kdim = a2.shape[1]
    tm = _pick(t, 1024)
    tk = _pick(kdim, 1024)
    n_exp = 0 if w_router is None else w_router.shape[1]
    in_specs = [
        pl.BlockSpec((tm, tk), lambda i, k: (i, k)),
        pl.BlockSpec((tk, d), lambda i, k: (k, 0)),
        pl.BlockSpec(memory_space=pl.ANY),
        pl.BlockSpec((1, d), lambda i, k: (0, 0)),
        pl.BlockSpec((1, d), lambda i, k: (0, 0)),
    ]
    args = [a2, w_o, x2, g.reshape(1, d), b.reshape(1, d)]
    out_shape = [jax.ShapeDtypeStruct((t, d), F32)]
    out_specs = [pl.BlockSpec((tm, d), lambda i, k: (i, 0))]
    if n_exp:
        assert MOE_TOPK == 2 and n_exp <= V7X_LANES
        wr = jnp.pad(w_router, ((0, 0), (0, V7X_LANES - n_exp))).astype(BF16)
        in_specs.append(pl.BlockSpec((d, V7X_LANES), lambda i, k: (0, 0)))
        args.append(wr)
        out_shape += [jax.ShapeDtypeStruct((t, V7X_LANES), jnp.int32), jax.ShapeDtypeStruct((t, V7X_LANES), F32)]
        out_specs += [pl.BlockSpec((tm, V7X_LANES), lambda i, k: (i, 0))] * 2
    kern = functools.partial(_oproj_ln_kernel, alpha=alpha, n_exp=n_exp)
    return pl.pallas_call(
        kern,
        out_shape=out_shape,
        grid=(t // tm, kdim // tk),
        in_specs=in_specs,
        out_specs=out_specs,
        scratch_shapes=[pltpu.VMEM((tm, d), F32), pltpu.SemaphoreType.DMA],
        compiler_params=_params(("parallel", "arbitrary")),
        name="oproj_ln",
    )(*args)


def _route(top_e, tm, n_exp):
    t = top_e.shape[0]
    n_assign = t * MOE_TOPK
    a_exp = top_e.reshape(-1)
    onehot = (a_exp[:, None] == jnp.arange(n_exp, dtype=jnp.int32)[None, :]).astype(jnp.int32)
    csum = jnp.cumsum(onehot, axis=0)
    rank = jnp.sum(csum * onehot, axis=1) - 1
    counts = csum[-1]
    p_counts = (counts + tm - 1) // tm * tm
    p_ends = jnp.cumsum(p_counts)
    p_starts = p_ends - p_counts
    dest = (p_starts[a_exp] + rank).astype(jnp.int32)
    n_blocks = -(-n_assign // tm) + n_exp
    p_rows = n_blocks * tm
    a_tok = jnp.arange(n_assign, dtype=jnp.int32) // MOE_TOPK
    buf_tok = jnp.zeros((p_rows,), jnp.int32).at[dest].set(a_tok)
    blk_start = jnp.arange(n_blocks, dtype=jnp.int32) * tm
    blk_exp = jnp.minimum(jnp.searchsorted(p_ends, blk_start, side="right"), n_exp - 1).astype(jnp.int32)
    blk_rows = jnp.clip(p_starts[blk_exp] + counts[blk_exp] - blk_start, 0, tm).astype(jnp.int32)
    n_used = (p_ends[-1] // tm).astype(jnp.int32).reshape(1)
    return buf_tok, blk_exp, blk_rows, n_used, dest.reshape(t, MOE_TOPK)


def _row_copy(src_hbm, row, dst_vmem, r, sem):
    return pltpu.make_async_copy(src_hbm.at[pl.ds(row, 1)], dst_vmem.at[pl.ds(r, 1)], sem)


def _moe_kernel(exp_ref, rows_ref, used_ref, tok_ref, x_hbm, wg_ref, wu_ref, wd_ref, o_ref,
                xg_ref, xb_ref, wgb_ref, wub_ref, wdb_ref, sem, *, rc):
    i = pl.program_id(0)
    j = pl.program_id(1)
    tm = xg_ref.shape[0]
    n_used = used_ref[0]
    active = i < n_used

    def start_gather(tile):
        for c0 in range(0, tm, rc):
            @pl.when(c0 < rows_ref[tile])
            def _(c0=c0):
                def start(r, c):
                    _row_copy(x_hbm, tok_ref[tile * tm + c0 + r], xg_ref, c0 + r, sem).start()
                    return c

                lax.fori_loop(0, rc, start, 0, unroll=8)

    @pl.when(jnp.logical_and(i == 0, j == 0))
    def _():
        start_gather(0)

    @pl.when(jnp.logical_and(j == 1, i + 1 < n_used))
    def _():
        start_gather(i + 1)

    @pl.when(j == 0)
    def _():
        o_ref[...] = jnp.zeros_like(o_ref)

    @pl.when(jnp.logical_and(active, j == 0))
    def _():
        for c0 in range(0, tm, rc):
            @pl.when(c0 < rows_ref[i])
            def _(c0=c0):
                rows = pl.ds(c0, rc)
                pltpu.make_async_copy(x_hbm.at[rows], xg_ref.at[rows], sem).wait()
        for c0 in range(0, tm, rc):
            @pl.when(c0 < rows_ref[i])
            def _(c0=c0):
                rows = pl.ds(c0, rc)
                xb_ref[rows, :] = xg_ref[rows, :].astype(BF16)

    @pl.when(active)
    def _():
        _ffn_accumulate(xb_ref, wg_ref.at[0], wu_ref.at[0], wd_ref.at[0], wgb_ref, wub_ref, wdb_ref, o_ref,
                        _pick(tm, 512), n_rows=rows_ref[i], rc_part=rc)


def _moe(x2, buf_tok, blk_exp, blk_rows, n_used, w_gate, w_up, w_down, tm):
    t, d = x2.shape
    n_exp, _, f = w_gate.shape
    tf = _pick(f, 256)
    rc = _pick(tm, 256)
    n_blocks = blk_exp.shape[0]
    nj = f // tf
    assert nj >= 2

    def w_idx(i, j, exp_ref, rows_ref, used_ref, tok_ref):
        last = used_ref[0] - 1
        return exp_ref[jnp.minimum(i, last)], jnp.where(i <= last, j, nj - 1)

    def wgu_map(i, j, *s):
        e, jj = w_idx(i, j, *s)
        return e, 0, jj

    def wd_map(i, j, *s):
        e, jj = w_idx(i, j, *s)
        return e, jj, 0

    grid_spec = pltpu.PrefetchScalarGridSpec(
        num_scalar_prefetch=4,
        grid=(n_blocks, nj),
        in_specs=[
            pl.BlockSpec(memory_space=pl.ANY),
            pl.BlockSpec((1, d, tf), wgu_map),
            pl.BlockSpec((1, d, tf), wgu_map),
            pl.BlockSpec((1, tf, d), wd_map),
        ],
        out_specs=pl.BlockSpec((tm, d), lambda i, j, *s: (i, 0)),
        scratch_shapes=[pltpu.VMEM((tm, d), F32), pltpu.VMEM((tm, d), BF16), pltpu.VMEM((d, tf), BF16),
                        pltpu.VMEM((d, tf), BF16), pltpu.VMEM((tf, d), BF16), pltpu.SemaphoreType.DMA],
    )
    return pl.pallas_call(
        functools.partial(_moe_kernel, rc=rc),
        out_shape=jax.ShapeDtypeStruct((n_blocks * tm, d), F32),
        grid_spec=grid_spec,
        compiler_params=_params(("arbitrary", "arbitrary")),
        name="moe",
    )(blk_exp, blk_rows, n_used, buf_tok, x2, w_gate, w_up, w_down)


def _combine_ln_kernel(p0_ref, p1_ref, x_ref, gate_ref, y_hbm, g_ref, b_ref, o_ref, ybuf_ref, sems, *, alpha):
    i = pl.program_id(0)
    tm = x_ref.shape[0]
    slot = i % 2

    def start_gather(tile, sl):
        def start(r, c):
            for k, p_ref in enumerate((p0_ref, p1_ref)):
                _row_copy(y_hbm, p_ref[tile * tm + r], ybuf_ref.at[sl, k], r, sems.at[sl, k]).start()
            return c

        lax.fori_loop(0, tm, start, 0, unroll=8)

    @pl.when(i == 0)
    def _():
        start_gather(0, 0)

    @pl.when(i + 1 < pl.num_programs(0))
    def _():
        start_gather(i + 1, 1 - slot)

    for k in range(MOE_TOPK):
        pltpu.make_async_copy(y_hbm.at[pl.ds(0, tm)], ybuf_ref.at[slot, k], sems.at[slot, k]).wait()
    gates = gate_ref[...]
    y = ybuf_ref[slot, 0] * gates[:, 0:1] + ybuf_ref[slot, 1] * gates[:, 1:2]
    o_ref[...] = _layer_norm(alpha * x_ref[...] + y, g_ref[...], b_ref[...])


def _combine_ln(x2, gates, y_sorted, pos, g, b, alpha):
    t, d = x2.shape
    tm = _pick(t, 256)
    grid_spec = pltpu.PrefetchScalarGridSpec(
        num_scalar_prefetch=2,
        grid=(t // tm,),
        in_specs=[
            pl.BlockSpec((tm, d), lambda i, *s: (i, 0)),
            pl.BlockSpec((tm, gates.shape[1]), lambda i, *s: (i, 0)),
            pl.BlockSpec(memory_space=pl.ANY),
            pl.BlockSpec((1, d), lambda i, *s: (0, 0)),
            pl.BlockSpec((1, d), lambda i, *s: (0, 0)),
        ],
        out_specs=pl.BlockSpec((tm, d), lambda i, *s: (i, 0)),
        scratch_shapes=[pltpu.VMEM((2, MOE_TOPK, tm, d), F32), pltpu.SemaphoreType.DMA((2, MOE_TOPK))],
    )
    kern = functools.partial(_combine_ln_kernel, alpha=alpha)
    return pl.pallas_call(
        kern,
        out_shape=jax.ShapeDtypeStruct((t, d), F32),
        grid_spec=grid_spec,
        compiler_params=_params(("arbitrary",)),
        name="combine_ln",
    )(pos[:, 0], pos[:, 1], x2, gates, y_sorted, g.reshape(1, d), b.reshape(1, d))


def _rope_tables(s):
    inv = 1.0 / (ROPE_THETA ** (jnp.arange(0, HEAD_DIM, 2, dtype=F32) / HEAD_DIM))
    ang = jnp.arange(s, dtype=F32)[:, None] * inv[None, :]
    ang = jnp.concatenate([ang, ang], axis=-1)
    sign = jnp.where(jnp.arange(HEAD_DIM) < HEAD_DIM // 2, -1.0, 1.0).astype(F32)
    return jnp.cos(ang), jnp.sin(ang) * sign[None, :]


def kernel(x, pool_w, pool_scale, w_kv, moba_wq, moba_wo, ffn_w_gate, ffn_w_up, ffn_w_down,
           moe_router, moe_w_gate, moe_w_up, moe_w_down, ln_mix_g, ln_mix_b, ln_ffn_g, ln_ffn_b):
    bsz, s, d = x.shape
    t = bsz * s
    depth = ln_mix_g.shape[0]
    n_a = pool_w.shape[0]
    alpha = (2.0 * depth) ** 0.25
    cos, sin_signed = _rope_tables(s)
    nb = s // MOBA_BLOCK
    assert s % MOBA_BLOCK == 0 and d % HEAD_DIM == 0
    kv = None
    x2 = x.reshape(t, d)
    xb = None
    for l in range(depth):
        if l < n_a:
            x3, xb3 = _pool_ln(x2.reshape(bsz, s, d), pool_w[l], pool_scale[l], ln_mix_g[l], ln_mix_b[l], alpha)
            x2, xb = x3.reshape(t, d), xb3.reshape(t, d)
            router = None
        else:
            bl = l - n_a
            k, vt, k_mean = kv
            xb = _to_bf16(x2) if xb is None else xb
            q = _proj_rope(xb, _to_bf16(moba_wq[bl]), 0, d, cos, sin_signed, s, with_mean=False,
                           out_scale=MOBA_Q_SCALE)[0]
            att = _moba(q.reshape(bsz, s, d), k.reshape(bsz, s, d), vt, k_mean)
            router = moe_router[l // 2] if l % 2 == 1 else None
            res = _oproj_ln(att.reshape(t, d), _to_bf16(moba_wo[bl]), x2, ln_mix_g[l], ln_mix_b[l], alpha, router)
            x2, xb = res[0], None
        jf = l // 2
        if l % 2 == 0:
            xb = _to_bf16(x2) if xb is None else xb
            x2, xb = _swiglu_ln(x2, xb, ffn_w_gate[jf], ffn_w_up[jf], ffn_w_down[jf], ln_ffn_g[l], ln_ffn_b[l],
                                alpha)
        else:
            n_exp = moe_router.shape[2]
            if router is None:
                raise NotImplementedError("MoE after a pooling mixer is not supported")
            tm = _pick(t * MOE_TOPK, 1024)
            buf_tok, blk_exp, blk_rows, n_used, pos = _route(res[1][:, :MOE_TOPK], tm, n_exp)
            y_sorted = _moe(x2, buf_tok, blk_exp, blk_rows, n_used,
                            moe_w_gate[jf], moe_w_up[jf], moe_w_down[jf], tm)
            x2, xb = _combine_ln(x2, res[2], y_sorted, pos, ln_ffn_g[l], ln_ffn_b[l], alpha), None
        if l == n_a - 1:
            xb = _to_bf16(x2) if xb is None else xb
            wkvb = _to_bf16(w_kv)
            k, k_mean = _proj_rope(xb, wkvb, 0, d, cos, sin_signed, s, with_mean=True)
            vt = _proj_vt(xb, wkvb, d, d, bsz, s)
            kv = (k, vt, k_mean.reshape(bsz, nb, d))
    return x2.reshape(bsz, s, d)
```

```python
import functools

import jax
import jax.numpy as jnp
from jax import lax
from jax.experimental import pallas as pl
from jax.experimental.pallas import tpu as pltpu

HEAD_DIM = 128
MOBA_BLOCK = 256
MOBA_TOPK = 3
ROPE_THETA = 10000.0
POOL_WINDOWS = (2, 4, 8, 16)
MOE_TOPK = 2
LN_EPS = 1e-5
NEG = -1e30
LOG2_E = 1.4426950408889634
MOBA_Q_SCALE = HEAD_DIM ** -0.5 * LOG2_E

V7X_LANES = 128
V7X_SUBLANES = 8
V7X_VMEM_LIMIT_BYTES = 56 * 1024 * 1024

F32 = jnp.float32
BF16 = jnp.bfloat16


def _dot(a, b):
    return jnp.dot(a, b, preferred_element_type=F32)


def _dot_nt(a, b):
    return lax.dot_general(a, b, (((1,), (1,)), ((), ())), preferred_element_type=F32)


def _layer_norm(y, g, b):
    mu = jnp.mean(y, axis=-1, keepdims=True)
    d = y - mu
    var = jnp.mean(d * d, axis=-1, keepdims=True)
    return d * lax.rsqrt(var + LN_EPS) * g + b


def _params(semantics):
    return pltpu.CompilerParams(dimension_semantics=semantics, vmem_limit_bytes=V7X_VMEM_LIMIT_BYTES)


def _pick(dim, pref):
    t = min(dim, pref)
    while dim % t:
        t //= 2
    return t


def _ln_rows(x_ref, o_ref, g_ref, b_ref, alpha, rc):
    tm = o_ref.shape[0]
    for c0 in range(0, tm, rc):
        rows = pl.ds(c0, min(rc, tm - c0))
        o_ref[rows, :] = _layer_norm(alpha * x_ref[rows, :] + o_ref[rows, :], g_ref[...], b_ref[...])


def _pool_ln_kernel(xh_ref, x_ref, w_ref, sc_ref, g_ref, b_ref, o_ref, ob_ref, *, alpha, windows, halo):
    i = pl.program_id(1)
    ts, d = x_ref.shape[1], x_ref.shape[2]
    c = d // len(windows)
    t_pos = i * ts + lax.broadcasted_iota(jnp.int32, (ts, 1), 0)
    for g, w in enumerate(windows):
        cols = pl.ds(g * c, c)
        x = x_ref[0, :, cols]
        prev = jnp.where(i > 0, xh_ref[0, :, cols], 0.0)
        cur = jnp.concatenate([prev, x], axis=0)
        width = 1
        while width < w:
            cur = cur + pltpu.roll(cur, width, axis=0)
            width *= 2
        cnt = jnp.minimum(t_pos + 1, w).astype(F32)
        diff = cur[halo:, :] / cnt - x
        y = _dot(diff.astype(BF16), w_ref[g].astype(BF16))
        o_ref[0, :, cols] = y * sc_ref[:, cols]
    _ln_rows(x_ref.at[0], o_ref.at[0], g_ref, b_ref, alpha, 256)
    ob_ref[0] = o_ref[0].astype(BF16)


def _pool_ln(x, w_groups, scale, g, b, alpha):
    bsz, s, d = x.shape
    windows = POOL_WINDOWS
    assert len(windows) == w_groups.shape[0] and d % len(windows) == 0
    assert all(w & (w - 1) == 0 for w in windows) and list(windows) == sorted(windows)
    halo = -(-max(windows) // V7X_SUBLANES) * V7X_SUBLANES
    ts = _pick(s, 512)
    assert ts % halo == 0
    r = ts // halo
    kern = functools.partial(_pool_ln_kernel, alpha=alpha, windows=windows, halo=halo)
    return pl.pallas_call(
        kern,
        out_shape=[jax.ShapeDtypeStruct(x.shape, F32), jax.ShapeDtypeStruct(x.shape, BF16)],
        grid=(bsz, s // ts),
        in_specs=[
            pl.BlockSpec((1, halo, d), lambda bi, i: (bi, jnp.maximum(i * r - 1, 0), 0)),
            pl.BlockSpec((1, ts, d), lambda bi, i: (bi, i, 0)),
            pl.BlockSpec(w_groups.shape, lambda bi, i: (0, 0, 0)),
            pl.BlockSpec((1, d), lambda bi, i: (0, 0)),
            pl.BlockSpec((1, d), lambda bi, i: (0, 0)),
            pl.BlockSpec((1, d), lambda bi, i: (0, 0)),
        ],
        out_specs=[pl.BlockSpec((1, ts, d), lambda bi, i: (bi, i, 0))] * 2,
        compiler_params=_params(("parallel", "parallel")),
        name="pool_ln",
    )(x, x, w_groups, scale.reshape(1, d), g.reshape(1, d), b.reshape(1, d))


def _ffn_accumulate(xb_ref, wg_ref, wu_ref, wd_ref, wgb_ref, wub_ref, wdb_ref, o_ref, rc, n_rows=None, rc_part=None):
    wgb_ref[...] = wg_ref[...].astype(BF16)
    wub_ref[...] = wu_ref[...].astype(BF16)
    wdb_ref[...] = wd_ref[...].astype(BF16)
    tm = xb_ref.shape[0]

    def chunk(c0, size):
        rows = pl.ds(c0, size)
        xb = xb_ref[rows, :]
        a = _dot(xb, wgb_ref[...])
        u = _dot(xb, wub_ref[...])
        h = (a * jax.nn.sigmoid(a)) * u
        o_ref[rows, :] += _dot(h.astype(BF16), wdb_ref[...])

    def full():
        for c0 in range(0, tm, rc):
            chunk(c0, rc)

    if n_rows is None:
        full()
        return
    pl.when(n_rows == tm)(full)
    n_chunks = (n_rows + rc_part - 1) // rc_part
    for live in range(1, tm // rc_part + 1):
        @pl.when(jnp.logical_and(n_rows < tm, n_chunks == live))
        def _(live=live):
            for c in range(live):
                chunk(c * rc_part, rc_part)


def _swiglu_ln_kernel(xb_ref, x_hbm, wg_ref, wu_ref, wd_ref, g_ref, b_ref, o_ref, ob_ref, wgb_ref, wub_ref, wdb_ref,
                      xr_ref, sems, *, alpha):
    i = pl.program_id(0)
    j = pl.program_id(1)
    tm = o_ref.shape[0]
    rl = xr_ref.shape[1]

    def x_copy(c):
        return pltpu.make_async_copy(x_hbm.at[pl.ds(i * tm + c * rl, rl)], xr_ref.at[c % 2], sems.at[c % 2])

    @pl.when(j == 0)
    def _():
        x_copy(0).start()
        o_ref[...] = jnp.zeros_like(o_ref)

    _ffn_accumulate(xb_ref, wg_ref, wu_ref, wd_ref, wgb_ref, wub_ref, wdb_ref, o_ref, _pick(tm, 512))

    @pl.when(j == pl.num_programs(1) - 1)
    def _():
        n_chunks = tm // rl
        for c in range(n_chunks):
            if c + 1 < n_chunks:
                x_copy(c + 1).start()
            x_copy(c).wait()
            rows = pl.ds(c * rl, rl)
            y = _layer_norm(alpha * xr_ref[c % 2] + o_ref[rows, :], g_ref[...], b_ref[...])
            o_ref[rows, :] = y
            ob_ref[rows, :] = y.astype(BF16)


def _swiglu_ln(x2, xb, w_gate, w_up, w_down, g, b, alpha):
    t, d = x2.shape
    f = w_gate.shape[1]
    tm = _pick(t, 1024)
    tf = _pick(f, 256)
    rl = _pick(tm, 128)
    kern = functools.partial(_swiglu_ln_kernel, alpha=alpha)
    return pl.pallas_call(
        kern,
        out_shape=[jax.ShapeDtypeStruct((t, d), F32), jax.ShapeDtypeStruct((t, d), BF16)],
        grid=(t // tm, f // tf),
        in_specs=[
            pl.BlockSpec((tm, d), lambda i, j: (i, 0)),
            pl.BlockSpec(memory_space=pl.ANY),
            pl.BlockSpec((d, tf), lambda i, j: (0, j)),
            pl.BlockSpec((d, tf), lambda i, j: (0, j)),
            pl.BlockSpec((tf, d), lambda i, j: (j, 0)),
            pl.BlockSpec((1, d), lambda i, j: (0, 0)),
            pl.BlockSpec((1, d), lambda i, j: (0, 0)),
        ],
        out_specs=[pl.BlockSpec((tm, d), lambda i, j: (i, 0)), pl.BlockSpec((tm, d), lambda i, j: (i, 0))],
        scratch_shapes=[pltpu.VMEM((d, tf), BF16), pltpu.VMEM((d, tf), BF16), pltpu.VMEM((tf, d), BF16),
                        pltpu.VMEM((2, rl, d), F32), pltpu.SemaphoreType.DMA((2,))],
        compiler_params=_params(("parallel", "arbitrary")),
        name="swiglu_ln",
    )(xb, x2, w_gate, w_up, w_down, g.reshape(1, d), b.reshape(1, d))


def _to_bf16_kernel(w_ref, o_ref):
    o_ref[...] = w_ref[...].astype(BF16)


def _to_bf16(w):
    r, c = w.shape
    tr = _pick(r, 512)
    return pl.pallas_call(
        _to_bf16_kernel,
        out_shape=jax.ShapeDtypeStruct((r, c), BF16),
        grid=(r // tr,),
        in_specs=[pl.BlockSpec((tr, c), lambda i: (i, 0))],
        out_specs=pl.BlockSpec((tr, c), lambda i: (i, 0)),
        compiler_params=_params(("parallel",)),
        name="to_bf16",
    )(w)


def _proj_rope_kernel(x_ref, w_ref, cos_ref, sin_ref, o_ref, *km_ref, blk, out_scale):
    tm, tn = o_ref.shape
    for c0 in range(0, tm, blk):
        rows = pl.ds(c0, blk)
        y = _dot(x_ref[rows, :], w_ref[...])
        cos = cos_ref[rows, :]
        sin = sin_ref[rows, :]
        if out_scale != 1.0:
            cos, sin = cos * out_scale, sin * out_scale
        heads = []
        for h in range(tn // HEAD_DIM):
            th = y[:, h * HEAD_DIM:(h + 1) * HEAD_DIM]
            heads.append(th * cos + pltpu.roll(th, HEAD_DIM // 2, axis=1) * sin)
        yr = heads[0] if len(heads) == 1 else jnp.concatenate(heads, axis=1)
        o_ref[rows, :] = yr.astype(BF16)
        if km_ref:
            km_ref[0][0, pl.ds(c0 // blk, 1), :] = jnp.sum(yr, axis=0, keepdims=True) * (1.0 / blk)


def _proj_rope(xb, wb, col0, n_out, cos, sin_signed, s, with_mean, out_scale=1.0):
    t, d = xb.shape
    tm = _pick(s, 1024)
    tn = _pick(n_out, 1024)
    assert tm % MOBA_BLOCK == 0 and tn % HEAD_DIM == 0 and col0 % tn == 0
    s_tiles = s // tm
    jb = col0 // tn
    out_shape = [jax.ShapeDtypeStruct((t, n_out), BF16)]
    out_specs = [pl.BlockSpec((tm, tn), lambda i, j: (i, j))]
    if with_mean:
        out_shape.append(jax.ShapeDtypeStruct((t // tm, tm // MOBA_BLOCK, n_out), F32))
        out_specs.append(pl.BlockSpec((1, tm // MOBA_BLOCK, tn), lambda i, j: (i, 0, j)))
    assert not (with_mean and out_scale != 1.0)
    kern = functools.partial(_proj_rope_kernel, blk=MOBA_BLOCK, out_scale=out_scale)
    return pl.pallas_call(
        kern,
        out_shape=out_shape,
        grid=(t // tm, n_out // tn),
        in_specs=[
            pl.BlockSpec((tm, d), lambda i, j: (i, 0)),
            pl.BlockSpec((d, tn), lambda i, j: (0, j + jb)),
            pl.BlockSpec((tm, HEAD_DIM), lambda i, j: (i % s_tiles, 0)),
            pl.BlockSpec((tm, HEAD_DIM), lambda i, j: (i % s_tiles, 0)),
        ],
        out_specs=out_specs,
        compiler_params=_params(("parallel", "arbitrary")),
        name="proj_rope_k" if with_mean else "proj_rope_q",
    )(xb, wb, cos, sin_signed)


def _proj_vt_kernel(x_ref, w_ref, o_ref, *, blk):
    tm = x_ref.shape[0]
    tn = w_ref.shape[1]
    for c in range(tm // blk):
        yt = _dot(x_ref[pl.ds(c * blk, blk), :], w_ref[...]).T
        for h in range(tn // HEAD_DIM):
            o_ref[0, h, c] = yt[h * HEAD_DIM:(h + 1) * HEAD_DIM, :].astype(BF16)


def _proj_vt(xb, wb, col0, n_out, bsz, s):
    t, d = xb.shape
    tm = _pick(s, 1024)
    tn = _pick(n_out, 1024)
    assert tm % MOBA_BLOCK == 0 and tn % HEAD_DIM == 0 and col0 % tn == 0
    s_tiles = s // tm
    jb = col0 // tn
    hpt = tn // HEAD_DIM
    nbt = tm // MOBA_BLOCK
    kern = functools.partial(_proj_vt_kernel, blk=MOBA_BLOCK)
    return pl.pallas_call(
        kern,
        out_shape=jax.ShapeDtypeStruct((bsz, n_out // HEAD_DIM, s // MOBA_BLOCK, HEAD_DIM, MOBA_BLOCK), BF16),
        grid=(t // tm, n_out // tn),
        in_specs=[
            pl.BlockSpec((tm, d), lambda i, j: (i, 0)),
            pl.BlockSpec((d, tn), lambda i, j: (0, j + jb)),
        ],
        out_specs=pl.BlockSpec((1, hpt, nbt, HEAD_DIM, MOBA_BLOCK),
                               lambda i, j: (i // s_tiles, j, i % s_tiles, 0, 0)),
        compiler_params=_params(("parallel", "arbitrary")),
        name="proj_vt",
    )(xb, wb)


def _moba_kernel(q_ref, k_ref, vt_ref, km_ref, o_ref, bias_ref, sa_ref, sb_ref, ma_ref, mb_ref, m8_ref, l8_ref,
                 acc_ref, **kw):
    even = pl.program_id(2) % 2 == 0
    args = (q_ref, k_ref, vt_ref, km_ref, o_ref, bias_ref, m8_ref, l8_ref, acc_ref)
    pl.when(even)(functools.partial(_moba_step, *args, sa_ref, ma_ref, sb_ref, mb_ref, **kw))
    pl.when(jnp.logical_not(even))(functools.partial(_moba_step, *args, sb_ref, mb_ref, sa_ref, ma_ref, **kw))


def _moba_step(q_ref, k_ref, vt_ref, km_ref, o_ref, bias_ref, m8_ref, l8_ref, acc_ref, s_cur, m_cur, s_prev, m_prev,
               *, topk, heads, unroll):
    i = pl.program_id(2)
    blk = q_ref.shape[1]
    nb = km_ref.shape[1]
    trips = (i + unroll - 1) // unroll
    ones = jnp.ones((V7X_SUBLANES, unroll * blk), BF16)

    def fold(x, op):
        return op(x.reshape(blk // V7X_SUBLANES, V7X_SUBLANES, blk), axis=0)

    def score_blocks(it):
        for un in range(unroll):
            n = it * unroll + un
            nk = jnp.minimum(n, nb - 1)
            rows = pl.ds(pl.multiple_of(nk * blk, blk), blk)
            for h in range(heads):
                hs = pl.ds(h * HEAD_DIM, HEAD_DIM)
                s_n = _dot_nt(k_ref[0, rows, hs], q_ref[0, :, hs]) + bias_ref[h, pl.ds(nk, 1), :]
                s_cur[h, n] = s_n
                m8_ref[h] = jnp.maximum(m8_ref[h], fold(s_n, jnp.max))

    def value_blocks(it):
        blocks = [it * unroll + un for un in range(unroll)]
        for h in range(heads):
            p = jnp.concatenate([jnp.exp2(s_prev[h, n] - m_prev[h]).astype(BF16) for n in blocks], axis=0)
            v = jnp.concatenate([vt_ref[0, h, jnp.minimum(n, nb - 1)] for n in blocks], axis=1)
            l8_ref[h] += _dot(ones, p)
            acc_ref[h] += _dot(v, p)

    @pl.when(i < nb)
    def _():
        blk_id = lax.broadcasted_iota(jnp.int32, (nb, blk), 0)
        past = blk_id < i
        for h in range(heads):
            hs = pl.ds(h * HEAD_DIM, HEAD_DIM)
            gate = _dot_nt(km_ref[0, :, hs].astype(BF16), q_ref[0, :, hs])
            gate = jnp.where(past, gate, NEG)
            rank = jnp.zeros((nb, blk), jnp.int32)
            for m in range(nb):
                gm = gate[m:m + 1, :]
                better = (gm > gate) | ((gm == gate) & (blk_id > m))
                rank = rank + better.astype(jnp.int32)
            bias_ref[h] = jnp.where(past & (rank < topk), 0.0, NEG)
            m8_ref[h] = jnp.full(m8_ref.shape[1:], NEG, F32)

    for h in range(heads):
        l8_ref[h] = jnp.zeros(l8_ref.shape[1:], F32)
        acc_ref[h] = jnp.zeros(acc_ref.shape[1:], F32)

    @pl.when(i < nb)
    def _():
        def both(it, carry):
            score_blocks(it)
            value_blocks(it)
            return carry

        lax.fori_loop(0, trips, both, 0)

    @pl.when(i == nb)
    def _():
        def only_values(it, carry):
            value_blocks(it)
            return carry

        lax.fori_loop(0, trips, only_values, 0)

    @pl.when(i < nb)
    def _():
        kpos = lax.broadcasted_iota(jnp.int32, (blk, blk), 0)
        qpos = lax.broadcasted_iota(jnp.int32, (blk, blk), 1)
        own_rows = pl.ds(pl.multiple_of(i * blk, blk), blk)
        for h in range(heads):
            hs = pl.ds(h * HEAD_DIM, HEAD_DIM)
            s = _dot_nt(k_ref[0, own_rows, hs], q_ref[0, :, hs])
            s = jnp.where(kpos <= qpos, s, NEG)
            s_cur[h, i] = s
            for un in range(1, unroll):
                s_cur[h, i + un] = jnp.full((blk, blk), NEG, F32)
            m_cur[h] = jnp.max(jnp.maximum(m8_ref[h], fold(s, jnp.max)), axis=0, keepdims=True)

    @pl.when(i > 0)
    def _():
        for h in range(heads):
            o_ref[0, :, pl.ds(h * HEAD_DIM, HEAD_DIM)] = (acc_ref[h] / l8_ref[h, 0:1, :]).T.astype(BF16)


def _moba(q, k, vt, k_mean):
    bsz, s, d = q.shape
    n_heads = d // HEAD_DIM
    nb = s // MOBA_BLOCK
    heads = _pick(n_heads, 4)
    unroll = 2
    gw = heads * HEAD_DIM
    kern = functools.partial(_moba_kernel, topk=min(MOBA_TOPK, nb), heads=heads, unroll=unroll)
    return pl.pallas_call(
        kern,
        out_shape=jax.ShapeDtypeStruct((bsz, s, d), BF16),
        grid=(bsz, n_heads // heads, nb + 1),
        in_specs=[
            pl.BlockSpec((1, MOBA_BLOCK, gw), lambda b, h, i: (b, jnp.minimum(i, nb - 1), h)),
            pl.BlockSpec((1, s, gw), lambda b, h, i: (b, 0, h)),
            pl.BlockSpec((1, heads, nb, HEAD_DIM, MOBA_BLOCK), lambda b, h, i: (b, h, 0, 0, 0)),
            pl.BlockSpec((1, nb, gw), lambda b, h, i: (b, 0, h)),
        ],
        out_specs=pl.BlockSpec((1, MOBA_BLOCK, gw), lambda b, h, i: (b, jnp.maximum(i - 1, 0), h)),
        scratch_shapes=[pltpu.VMEM((heads, nb, MOBA_BLOCK), F32),
                        pltpu.VMEM((heads, nb + unroll - 1, MOBA_BLOCK, MOBA_BLOCK), F32),
                        pltpu.VMEM((heads, nb + unroll - 1, MOBA_BLOCK, MOBA_BLOCK), F32),
                        pltpu.VMEM((heads, 1, MOBA_BLOCK), F32),
                        pltpu.VMEM((heads, 1, MOBA_BLOCK), F32),
                        pltpu.VMEM((heads, V7X_SUBLANES, MOBA_BLOCK), F32),
                        pltpu.VMEM((heads, V7X_SUBLANES, MOBA_BLOCK), F32),
                        pltpu.VMEM((heads, HEAD_DIM, MOBA_BLOCK), F32)],
        compiler_params=_params(("parallel", "parallel", "arbitrary")),
        name="moba",
    )(q, k, vt, k_mean)


def _oproj_ln_kernel(a_ref, w_ref, x_hbm, g_ref, b_ref, *refs, alpha, n_exp):
    if n_exp:
        wr_ref, o_ref, idx_ref, gate_ref, x_ref, sem = refs
    else:
        o_ref, x_ref, sem = refs
    kk = pl.program_id(1)
    tm = o_ref.shape[0]
    rc = min(tm, 256)
    x_copy = pltpu.make_async_copy(x_hbm.at[pl.ds(pl.program_id(0) * tm, tm)], x_ref, sem)

    @pl.when(kk == 0)
    def _():
        x_copy.start()
        o_ref[...] = jnp.zeros_like(o_ref)

    for c0 in range(0, tm, rc):
        rows = pl.ds(c0, rc)
        o_ref[rows, :] += _dot(a_ref[rows, :], w_ref[...])

    @pl.when(kk == pl.num_programs(1) - 1)
    def _():
        x_copy.wait()
        for c0 in range(0, tm, rc):
            rows = pl.ds(c0, rc)
            y = _layer_norm(alpha * x_ref[rows, :] + o_ref[rows, :], g_ref[...], b_ref[...])
            o_ref[rows, :] = y
            if n_exp:
                logits = _dot(y.astype(BF16), wr_ref[...])
                lane = lax.broadcasted_iota(jnp.int32, logits.shape, 1)
                big = logits.shape[1]
                logits = jnp.where(lane < n_exp, logits, -jnp.inf)
                m1 = jnp.max(logits, axis=1, keepdims=True)
                i1 = jnp.min(jnp.where(logits == m1, lane, big), axis=1, keepdims=True)
                rest = jnp.where(lane == i1, -jnp.inf, logits)
                m2 = jnp.max(rest, axis=1, keepdims=True)
                i2 = jnp.min(jnp.where(rest == m2, lane, big), axis=1, keepdims=True)
                e2 = jnp.exp(m2 - m1)
                den = 1.0 + e2
                idx_ref[rows, :] = jnp.where(lane == 0, i1, jnp.where(lane == 1, i2, 0))
                gate_ref[rows, :] = jnp.where(lane == 0, 1.0 / den, jnp.where(lane == 1, e2 / den, 0.0))


def _oproj_ln(a2, w_o, x2, g, b, alpha, w_router=None):
    t, d = x2.shape
    kdim = a2.shape[1]
    tm = _pick(t, 1024)
    tk = _pick(kdim, 1024)
    n_exp = 0 if w_router is None else w_router.shape[1]
    in_specs = [
        pl.BlockSpec((tm, tk), lambda i, k: (i, k)),
        pl.BlockSpec((tk, d), lambda i, k: (k, 0)),
        pl.BlockSpec(memory_space=pl.ANY),
        pl.BlockSpec((1, d), lambda i, k: (0, 0)),
        pl.BlockSpec((1, d), lambda i, k: (0, 0)),
    ]
    args = [a2, w_o, x2, g.reshape(1, d), b.reshape(1, d)]
    out_shape = [jax.ShapeDtypeStruct((t, d), F32)]
    out_specs = [pl.BlockSpec((tm, d), lambda i, k: (i, 0))]
    if n_exp:
        assert MOE_TOPK == 2 and n_exp <= V7X_LANES
        wr = jnp.pad(w_router, ((0, 0), (0, V7X_LANES - n_exp))).astype(BF16)
        in_specs.append(pl.BlockSpec((d, V7X_LANES), lambda i, k: (0, 0)))
        args.append(wr)
        out_shape += [jax.ShapeDtypeStruct((t, V7X_LANES), jnp.int32), jax.ShapeDtypeStruct((t, V7X_LANES), F32)]
        out_specs += [pl.BlockSpec((tm, V7X_LANES), lambda i, k: (i, 0))] * 2
    kern = functools.partial(_oproj_ln_kernel, alpha=alpha, n_exp=n_exp)
    return pl.pallas_call(
        kern,
        out_shape=out_shape,
        grid=(t // tm, kdim // tk),
        in_specs=in_specs,
        out_specs=out_specs,
        scratch_shapes=[pltpu.VMEM((tm, d), F32), pltpu.SemaphoreType.DMA],
        compiler_params=_params(("parallel", "arbitrary")),
        name="oproj_ln",
    )(*args)


def _route(top_e, tm, n_exp):
    t = top_e.shape[0]
    n_assign = t * MOE_TOPK
    a_exp = top_e.reshape(-1)
    onehot = (a_exp[:, None] == jnp.arange(n_exp, dtype=jnp.int32)[None, :]).astype(jnp.int32)
    csum = jnp.cumsum(onehot, axis=0)
    rank = jnp.sum(csum * onehot, axis=1) - 1
    counts = csum[-1]
    p_counts = (counts + tm - 1) // tm * tm
    p_ends = jnp.cumsum(p_counts)
    p_starts = p_ends - p_counts
    dest = (p_starts[a_exp] + rank).astype(jnp.int32)
    n_blocks = -(-n_assign // tm) + n_exp
    p_rows = n_blocks * tm
    a_tok = jnp.arange(n_assign, dtype=jnp.int32) // MOE_TOPK
    buf_tok = jnp.zeros((p_rows,), jnp.int32).at[dest].set(a_tok)
    blk_start = jnp.arange(n_blocks, dtype=jnp.int32) * tm
    blk_exp = jnp.minimum(jnp.searchsorted(p_ends, blk_start, side="right"), n_exp - 1).astype(jnp.int32)
    blk_rows = jnp.clip(p_starts[blk_exp] + counts[blk_exp] - blk_start, 0, tm).astype(jnp.int32)
    n_used = (p_ends[-1] // tm).astype(jnp.int32).reshape(1)
    return buf_tok, blk_exp, blk_rows, n_used, dest.reshape(t, MOE_TOPK)


def _row_copy(src_hbm, row, dst_vmem, r, sem):
    return pltpu.make_async_copy(src_hbm.at[pl.ds(row, 1)], dst_vmem.at[pl.ds(r, 1)], sem)


def _moe_kernel(exp_ref, rows_ref, used_ref, tok_ref, x_hbm, wg_ref, wu_ref, wd_ref, o_ref,
                xg_ref, xb_ref, wgb_ref, wub_ref, wdb_ref, sem, *, rc):
    i = pl.program_id(0)
    j = pl.program_id(1)
    tm = xg_ref.shape[0]
    n_used = used_ref[0]
    active = i < n_used

    def start_gather(tile):
        for c0 in range(0, tm, rc):
            @pl.when(c0 < rows_ref[tile])
            def _(c0=c0):
                def start(r, c):
                    _row_copy(x_hbm, tok_ref[tile * tm + c0 + r], xg_ref, c0 + r, sem).start()
                    return c

                lax.fori_loop(0, rc, start, 0, unroll=8)

    @pl.when(jnp.logical_and(i == 0, j == 0))
    def _():
        start_gather(0)

    @pl.when(jnp.logical_and(j == 1, i + 1 < n_used))
    def _():
        start_gather(i + 1)

    @pl.when(j == 0)
    def _():
        o_ref[...] = jnp.zeros_like(o_ref)

    @pl.when(jnp.logical_and(active, j == 0))
    def _():
        for c0 in range(0, tm, rc):
            @pl.when(c0 < rows_ref[i])
            def _(c0=c0):
                rows = pl.ds(c0, rc)
                pltpu.make_async_copy(x_hbm.at[rows], xg_ref.at[rows], sem).wait()
        for c0 in range(0, tm, rc):
            @pl.when(c0 < rows_ref[i])
            def _(c0=c0):
                rows = pl.ds(c0, rc)
                xb_ref[rows, :] = xg_ref[rows, :].astype(BF16)

    @pl.when(active)
    def _():
        _ffn_accumulate(xb_ref, wg_ref.at[0], wu_ref.at[0], wd_ref.at[0], wgb_ref, wub_ref, wdb_ref, o_ref,
                        _pick(tm, 512), n_rows=rows_ref[i], rc_part=rc)


def _moe(x2, buf_tok, blk_exp, blk_rows, n_used, w_gate, w_up, w_down, tm):
    t, d = x2.shape
    n_exp, _, f = w_gate.shape
    tf = _pick(f, 256)
    rc = _pick(tm, 256)
    n_blocks = blk_exp.shape[0]
    nj = f // tf
    assert nj >= 2

    def w_idx(i, j, exp_ref, rows_ref, used_ref, tok_ref):
        last = used_ref[0] - 1
        return exp_ref[jnp.minimum(i, last)], jnp.where(i <= last, j, nj - 1)

    def wgu_map(i, j, *s):
        e, jj = w_idx(i, j, *s)
        return e, 0, jj

    def wd_map(i, j, *s):
        e, jj = w_idx(i, j, *s)
        return e, jj, 0

    grid_spec = pltpu.PrefetchScalarGridSpec(
        num_scalar_prefetch=4,
        grid=(n_blocks, nj),
        in_specs=[
            pl.BlockSpec(memory_space=pl.ANY),
            pl.BlockSpec((1, d, tf), wgu_map),
            pl.BlockSpec((1, d, tf), wgu_map),
            pl.BlockSpec((1, tf, d), wd_map),
        ],
        out_specs=pl.BlockSpec((tm, d), lambda i, j, *s: (i, 0)),
        scratch_shapes=[pltpu.VMEM((tm, d), F32), pltpu.VMEM((tm, d), BF16), pltpu.VMEM((d, tf), BF16),
                        pltpu.VMEM((d, tf), BF16), pltpu.VMEM((tf, d), BF16), pltpu.SemaphoreType.DMA],
    )
    return pl.pallas_call(
        functools.partial(_moe_kernel, rc=rc),
        out_shape=jax.ShapeDtypeStruct((n_blocks * tm, d), F32),
        grid_spec=grid_spec,
        compiler_params=_params(("arbitrary", "arbitrary")),
        name="moe",
    )(blk_exp, blk_rows, n_used, buf_tok, x2, w_gate, w_up, w_down)


def _combine_ln_kernel(p0_ref, p1_ref, x_ref, gate_ref, y_hbm, g_ref, b_ref, o_ref, ybuf_ref, sems, *, alpha):
    i = pl.program_id(0)
    tm = x_ref.shape[0]
    slot = i % 2

    def start_gather(tile, sl):
        def start(r, c):
            for k, p_ref in enumerate((p0_ref, p1_ref)):
                _row_copy(y_hbm, p_ref[tile * tm + r], ybuf_ref.at[sl, k], r, sems.at[sl, k]).start()
            return c

        lax.fori_loop(0, tm, start, 0, unroll=8)

    @pl.when(i == 0)
    def _():
        start_gather(0, 0)

    @pl.when(i + 1 < pl.num_programs(0))
    def _():
        start_gather(i + 1, 1 - slot)

    for k in range(MOE_TOPK):
        pltpu.make_async_copy(y_hbm.at[pl.ds(0, tm)], ybuf_ref.at[slot, k], sems.at[slot, k]).wait()
    rc = _pick(tm, 256)
    for c0 in range(0, tm, rc):
        rows = pl.ds(c0, rc)
        gates = gate_ref[rows, :]
        y = ybuf_ref[slot, 0, rows, :] * gates[:, 0:1] + ybuf_ref[slot, 1, rows, :] * gates[:, 1:2]
        o_ref[rows, :] = _layer_norm(alpha * x_ref[rows, :] + y, g_ref[...], b_ref[...])


def _combine_ln(x2, gates, y_sorted, pos, g, b, alpha):
    t, d = x2.shape
    tm = _pick(t, 512)
    grid_spec = pltpu.PrefetchScalarGridSpec(
        num_scalar_prefetch=2,
        grid=(t // tm,),
        in_specs=[
            pl.BlockSpec((tm, d), lambda i, *s: (i, 0)),
            pl.BlockSpec((tm, gates.shape[1]), lambda i, *s: (i, 0)),
            pl.BlockSpec(memory_space=pl.ANY),
            pl.BlockSpec((1, d), lambda i, *s: (0, 0)),
            pl.BlockSpec((1, d), lambda i, *s: (0, 0)),
        ],
        out_specs=pl.BlockSpec((tm, d), lambda i, *s: (i, 0)),
        scratch_shapes=[pltpu.VMEM((2, MOE_TOPK, tm, d), F32), pltpu.SemaphoreType.DMA((2, MOE_TOPK))],
    )
    kern = functools.partial(_combine_ln_kernel, alpha=alpha)
    return pl.pallas_call(
        kern,
        out_shape=jax.ShapeDtypeStruct((t, d), F32),
        grid_spec=grid_spec,
        compiler_params=_params(("arbitrary",)),
        name="combine_ln",
    )(pos[:, 0], pos[:, 1], x2, gates, y_sorted, g.reshape(1, d), b.reshape(1, d))


def _rope_tables(s):
    inv = 1.0 / (ROPE_THETA ** (jnp.arange(0, HEAD_DIM, 2, dtype=F32) / HEAD_DIM))
    ang = jnp.arange(s, dtype=F32)[:, None] * inv[None, :]
    ang = jnp.concatenate([ang, ang], axis=-1)
    sign = jnp.where(jnp.arange(HEAD_DIM) < HEAD_DIM // 2, -1.0, 1.0).astype(F32)
    return jnp.cos(ang), jnp.sin(ang) * sign[None, :]


def kernel(x, pool_w, pool_scale, w_kv, moba_wq, moba_wo, ffn_w_gate, ffn_w_up, ffn_w_down,
           moe_router, moe_w_gate, moe_w_up, moe_w_down, ln_mix_g, ln_mix_b, ln_ffn_g, ln_ffn_b):
    bsz, s, d = x.shape
    t = bsz * s
    depth = ln_mix_g.shape[0]
    n_a = pool_w.shape[0]
    alpha = (2.0 * depth) ** 0.25
    cos, sin_signed = _rope_tables(s)
    nb = s // MOBA_BLOCK
    assert s % MOBA_BLOCK == 0 and d % HEAD_DIM == 0
    kv = None
    x2 = x.reshape(t, d)
    xb = None
    for l in range(depth):
        if l < n_a:
            x3, xb3 = _pool_ln(x2.reshape(bsz, s, d), pool_w[l], pool_scale[l], ln_mix_g[l], ln_mix_b[l], alpha)
            x2, xb = x3.reshape(t, d), xb3.reshape(t, d)
            router = None
        else:
            bl = l - n_a
            k, vt, k_mean = kv
            xb = _to_bf16(x2) if xb is None else xb
            q = _proj_rope(xb, _to_bf16(moba_wq[bl]), 0, d, cos, sin_signed, s, with_mean=False,
                           out_scale=MOBA_Q_SCALE)[0]
            att = _moba(q.reshape(bsz, s, d), k.reshape(bsz, s, d), vt, k_mean)
            router = moe_router[l // 2] if l % 2 == 1 else None
            res = _oproj_ln(att.reshape(t, d), _to_bf16(moba_wo[bl]), x2, ln_mix_g[l], ln_mix_b[l], alpha, router)
            x2, xb = res[0], None
        jf = l // 2
        if l % 2 == 0:
            xb = _to_bf16(x2) if xb is None else xb
            x2, xb = _swiglu_ln(x2, xb, ffn_w_gate[jf], ffn_w_up[jf], ffn_w_down[jf], ln_ffn_g[l], ln_ffn_b[l],
                                alpha)
        else:
            n_exp = moe_router.shape[2]
            if router is None:
                raise NotImplementedError("MoE after a pooling mixer is not supported")
            tm = _pick(t * MOE_TOPK, 1024)
            buf_tok, blk_exp, blk_rows, n_used, pos = _route(res[1][:, :MOE_TOPK], tm, n_exp)
            y_sorted = _moe(x2, buf_tok, blk_exp, blk_rows, n_used,
                            moe_w_gate[jf], moe_w_up[jf], moe_w_down[jf], tm)
            x2, xb = _combine_ln(x2, res[2], y_sorted, pos, ln_ffn_g[l], ln_ffn_b[l], alpha), None
        if l == n_a - 1:
            xb = _to_bf16(x2) if xb is None else xb
            wkvb = _to_bf16(w_kv)
            k, k_mean = _proj_rope(xb, wkvb, 0, d, cos, sin_signed, s, with_mean=True)
            vt = _proj_vt(xb, wkvb, d, d, bsz, s)
            kv = (k, vt, k_mean.reshape(bsz, nb, d))
    return x2.reshape(bsz, s, d)
```

```python
import functools

import jax
import jax.numpy as jnp
from jax import lax
from jax.experimental import pallas as pl
from jax.experimental.pallas import tpu as pltpu

HEAD_DIM = 128
MOBA_BLOCK = 256
MOBA_TOPK = 3
ROPE_THETA = 10000.0
POOL_WINDOWS = (2, 4, 8, 16)
MOE_TOPK = 2
LN_EPS = 1e-5
NEG = -1e30
LOG2_E = 1.4426950408889634
MOBA_Q_SCALE = HEAD_DIM ** -0.5 * LOG2_E

V7X_LANES = 128
V7X_SUBLANES = 8
V7X_VMEM_LIMIT_BYTES = 56 * 1024 * 1024

F32 = jnp.float32
BF16 = jnp.bfloat16


def _dot(a, b):
    return jnp.dot(a, b, preferred_element_type=F32)


def _dot_nt(a, b):
    return lax.dot_general(a, b, (((1,), (1,)), ((), ())), preferred_element_type=F32)


def _layer_norm(y, g, b):
    mu = jnp.mean(y, axis=-1, keepdims=True)
    d = y - mu
    var = jnp.mean(d * d, axis=-1, keepdims=True)
    return d * lax.rsqrt(var + LN_EPS) * g + b


def _params(semantics):
    return pltpu.CompilerParams(dimension_semantics=semantics, vmem_limit_bytes=V7X_VMEM_LIMIT_BYTES)


def _pick(dim, pref):
    t = min(dim, pref)
    while dim % t:
        t //= 2
    return t


def _ln_rows(x_ref, o_ref, g_ref, b_ref, alpha, rc):
    tm = o_ref.shape[0]
    for c0 in range(0, tm, rc):
        rows = pl.ds(c0, min(rc, tm - c0))
        o_ref[rows, :] = _layer_norm(alpha * x_ref[rows, :] + o_ref[rows, :], g_ref[...], b_ref[...])


def _pool_ln_kernel(xh_ref, x_ref, w_ref, sc_ref, g_ref, b_ref, o_ref, ob_ref, *, alpha, windows, halo):
    i = pl.program_id(1)
    ts, d = x_ref.shape[1], x_ref.shape[2]
    c = d // len(windows)
    t_pos = i * ts + lax.broadcasted_iota(jnp.int32, (ts, 1), 0)
    for g, w in enumerate(windows):
        cols = pl.ds(g * c, c)
        x = x_ref[0, :, cols]
        prev = jnp.where(i > 0, xh_ref[0, :, cols], 0.0)
        cur = jnp.concatenate([prev, x], axis=0)
        width = 1
        while width < w:
            cur = cur + pltpu.roll(cur, width, axis=0)
            width *= 2
        cnt = jnp.minimum(t_pos + 1, w).astype(F32)
        diff = cur[halo:, :] / cnt - x
        y = _dot(diff.astype(BF16), w_ref[g].astype(BF16))
        o_ref[0, :, cols] = y * sc_ref[:, cols]
    _ln_rows(x_ref.at[0], o_ref.at[0], g_ref, b_ref, alpha, 256)
    ob_ref[0] = o_ref[0].astype(BF16)


def _pool_ln(x, w_groups, scale, g, b, alpha):
    bsz, s, d = x.shape
    windows = POOL_WINDOWS
    assert len(windows) == w_groups.shape[0] and d % len(windows) == 0
    assert all(w & (w - 1) == 0 for w in windows) and list(windows) == sorted(windows)
    halo = -(-max(windows) // V7X_SUBLANES) * V7X_SUBLANES
    ts = _pick(s, 512)
    assert ts % halo == 0
    r = ts // halo
    kern = functools.partial(_pool_ln_kernel, alpha=alpha, windows=windows, halo=halo)
    return pl.pallas_call(
        kern,
        out_shape=[jax.ShapeDtypeStruct(x.shape, F32), jax.ShapeDtypeStruct(x.shape, BF16)],
        grid=(bsz, s // ts),
        in_specs=[
            pl.BlockSpec((1, halo, d), lambda bi, i: (bi, jnp.maximum(i * r - 1, 0), 0)),
            pl.BlockSpec((1, ts, d), lambda bi, i: (bi, i, 0)),
            pl.BlockSpec(w_groups.shape, lambda bi, i: (0, 0, 0)),
            pl.BlockSpec((1, d), lambda bi, i: (0, 0)),
            pl.BlockSpec((1, d), lambda bi, i: (0, 0)),
            pl.BlockSpec((1, d), lambda bi, i: (0, 0)),
        ],
        out_specs=[pl.BlockSpec((1, ts, d), lambda bi, i: (bi, i, 0))] * 2,
        compiler_params=_params(("parallel", "parallel")),
        name="pool_ln",
    )(x, x, w_groups, scale.reshape(1, d), g.reshape(1, d), b.reshape(1, d))


def _ffn_accumulate(xb_ref, wg_ref, wu_ref, wd_ref, wgb_ref, wub_ref, wdb_ref, o_ref, rc, n_rows=None, rc_part=None):
    wgb_ref[...] = wg_ref[...].astype(BF16)
    wub_ref[...] = wu_ref[...].astype(BF16)
    wdb_ref[...] = wd_ref[...].astype(BF16)
    tm = xb_ref.shape[0]

    def chunk(c0, size):
        rows = pl.ds(c0, size)
        xb = xb_ref[rows, :]
        a = _dot(xb, wgb_ref[...])
        u = _dot(xb, wub_ref[...])
        h = (a * jax.nn.sigmoid(a)) * u
        o_ref[rows, :] += _dot(h.astype(BF16), wdb_ref[...])

    def full():
        for c0 in range(0, tm, rc):
            chunk(c0, rc)

    if n_rows is None:
        full()
        return
    pl.when(n_rows == tm)(full)
    n_chunks = (n_rows + rc_part - 1) // rc_part
    for live in range(1, tm // rc_part + 1):
        @pl.when(jnp.logical_and(n_rows < tm, n_chunks == live))
        def _(live=live):
            for c in range(live):
                chunk(c * rc_part, rc_part)


def _swiglu_ln_kernel(xb_ref, x_hbm, wg_ref, wu_ref, wd_ref, g_ref, b_ref, o_ref, ob_ref, wgb_ref, wub_ref, wdb_ref,
                      xr_ref, sems, *, alpha):
    i = pl.program_id(0)
    j = pl.program_id(1)
    tm = o_ref.shape[0]
    rl = xr_ref.shape[1]

    def x_copy(c):
        return pltpu.make_async_copy(x_hbm.at[pl.ds(i * tm + c * rl, rl)], xr_ref.at[c % 2], sems.at[c % 2])

    @pl.when(j == 0)
    def _():
        x_copy(0).start()
        o_ref[...] = jnp.zeros_like(o_ref)

    _ffn_accumulate(xb_ref, wg_ref, wu_ref, wd_ref, wgb_ref, wub_ref, wdb_ref, o_ref, _pick(tm, 512))

    @pl.when(j == pl.num_programs(1) - 1)
    def _():
        n_chunks = tm // rl
        for c in range(n_chunks):
            if c + 1 < n_chunks:
                x_copy(c + 1).start()
            x_copy(c).wait()
            rows = pl.ds(c * rl, rl)
            y = _layer_norm(alpha * xr_ref[c % 2] + o_ref[rows, :], g_ref[...], b_ref[...])
            o_ref[rows, :] = y
            ob_ref[rows, :] = y.astype(BF16)


def _swiglu_ln(x2, xb, w_gate, w_up, w_down, g, b, alpha):
    t, d = x2.shape
    f = w_gate.shape[1]
    tm = _pick(t, 1024)
    tf = _pick(f, 256)
    rl = _pick(tm, 128)
    kern = functools.partial(_swiglu_ln_kernel, alpha=alpha)
    return pl.pallas_call(
        kern,
        out_shape=[jax.ShapeDtypeStruct((t, d), F32), jax.ShapeDtypeStruct((t, d), BF16)],
        grid=(t // tm, f // tf),
        in_specs=[
            pl.BlockSpec((tm, d), lambda i, j: (i, 0)),
            pl.BlockSpec(memory_space=pl.ANY),
            pl.BlockSpec((d, tf), lambda i, j: (0, j)),
            pl.BlockSpec((d, tf), lambda i, j: (0, j)),
            pl.BlockSpec((tf, d), lambda i, j: (j, 0)),
            pl.BlockSpec((1, d), lambda i, j: (0, 0)),
            pl.BlockSpec((1, d), lambda i, j: (0, 0)),
        ],
        out_specs=[pl.BlockSpec((tm, d), lambda i, j: (i, 0)), pl.BlockSpec((tm, d), lambda i, j: (i, 0))],
        scratch_shapes=[pltpu.VMEM((d, tf), BF16), pltpu.VMEM((d, tf), BF16), pltpu.VMEM((tf, d), BF16),
                        pltpu.VMEM((2, rl, d), F32), pltpu.SemaphoreType.DMA((2,))],
        compiler_params=_params(("parallel", "arbitrary")),
        name="swiglu_ln",
    )(xb, x2, w_gate, w_up, w_down, g.reshape(1, d), b.reshape(1, d))


def _to_bf16_kernel(w_ref, o_ref):
    o_ref[...] = w_ref[...].astype(BF16)


def _to_bf16(w):
    r, c = w.shape
    tr = _pick(r, 512)
    return pl.pallas_call(
        _to_bf16_kernel,
        out_shape=jax.ShapeDtypeStruct((r, c), BF16),
        grid=(r // tr,),
        in_specs=[pl.BlockSpec((tr, c), lambda i: (i, 0))],
        out_specs=pl.BlockSpec((tr, c), lambda i: (i, 0)),
        compiler_params=_params(("parallel",)),
        name="to_bf16",
    )(w)


def _proj_rope_kernel(x_ref, w_ref, cos_ref, sin_ref, o_ref, *km_ref, blk, out_scale):
    tm, tn = o_ref.shape
    for c0 in range(0, tm, blk):
        rows = pl.ds(c0, blk)
        y = _dot(x_ref[rows, :], w_ref[...])
        cos = cos_ref[rows, :]
        sin = sin_ref[rows, :]
        if out_scale != 1.0:
            cos, sin = cos * out_scale, sin * out_scale
        heads = []
        for h in range(tn // HEAD_DIM):
            th = y[:, h * HEAD_DIM:(h + 1) * HEAD_DIM]
            heads.append(th * cos + pltpu.roll(th, HEAD_DIM // 2, axis=1) * sin)
        yr = heads[0] if len(heads) == 1 else jnp.concatenate(heads, axis=1)
        o_ref[rows, :] = yr.astype(BF16)
        if km_ref:
            km_ref[0][0, pl.ds(c0 // blk, 1), :] = jnp.sum(yr, axis=0, keepdims=True) * (1.0 / blk)


def _proj_rope(xb, wb, col0, n_out, cos, sin_signed, s, with_mean, out_scale=1.0):
    t, d = xb.shape
    tm = _pick(s, 1024)
    tn = _pick(n_out, 1024)
    assert tm % MOBA_BLOCK == 0 and tn % HEAD_DIM == 0 and col0 % tn == 0
    s_tiles = s // tm
    jb = col0 // tn
    out_shape = [jax.ShapeDtypeStruct((t, n_out), BF16)]
    out_specs = [pl.BlockSpec((tm, tn), lambda i, j: (i, j))]
    if with_mean:
        out_shape.append(jax.ShapeDtypeStruct((t // tm, tm // MOBA_BLOCK, n_out), F32))
        out_specs.append(pl.BlockSpec((1, tm // MOBA_BLOCK, tn), lambda i, j: (i, 0, j)))
    assert not (with_mean and out_scale != 1.0)
    kern = functools.partial(_proj_rope_kernel, blk=MOBA_BLOCK, out_scale=out_scale)
    return pl.pallas_call(
        kern,
        out_shape=out_shape,
        grid=(t // tm, n_out // tn),
        in_specs=[
            pl.BlockSpec((tm, d), lambda i, j: (i, 0)),
            pl.BlockSpec((d, tn), lambda i, j: (0, j + jb)),
            pl.BlockSpec((tm, HEAD_DIM), lambda i, j: (i % s_tiles, 0)),
            pl.BlockSpec((tm, HEAD_DIM), lambda i, j: (i % s_tiles, 0)),
        ],
        out_specs=out_specs,
        compiler_params=_params(("parallel", "arbitrary")),
        name="proj_rope_k" if with_mean else "proj_rope_q",
    )(xb, wb, cos, sin_signed)


def _proj_vt_kernel(x_ref, w_ref, o_ref, *, blk):
    tm = x_ref.shape[0]
    tn = w_ref.shape[1]
    for c in range(tm // blk):
        yt = _dot(x_ref[pl.ds(c * blk, blk), :], w_ref[...]).T
        for h in range(tn // HEAD_DIM):
            o_ref[0, h, c] = yt[h * HEAD_DIM:(h + 1) * HEAD_DIM, :].astype(BF16)


def _proj_vt(xb, wb, col0, n_out, bsz, s):
    t, d = xb.shape
    tm = _pick(s, 1024)
    tn = _pick(n_out, 1024)
    assert tm % MOBA_BLOCK == 0 and tn % HEAD_DIM == 0 and col0 % tn == 0
    s_tiles = s // tm
    jb = col0 // tn
    hpt = tn // HEAD_DIM
    nbt = tm // MOBA_BLOCK
    kern = functools.partial(_proj_vt_kernel, blk=MOBA_BLOCK)
    return pl.pallas_call(
        kern,
        out_shape=jax.ShapeDtypeStruct((bsz, n_out // HEAD_DIM, s // MOBA_BLOCK, HEAD_DIM, MOBA_BLOCK), BF16),
        grid=(t // tm, n_out // tn),
        in_specs=[
            pl.BlockSpec((tm, d), lambda i, j: (i, 0)),
            pl.BlockSpec((d, tn), lambda i, j: (0, j + jb)),
        ],
        out_specs=pl.BlockSpec((1, hpt, nbt, HEAD_DIM, MOBA_BLOCK),
                               lambda i, j: (i // s_tiles, j, i % s_tiles, 0, 0)),
        compiler_params=_params(("parallel", "arbitrary")),
        name="proj_vt",
    )(xb, wb)


def _moba_kernel(q_ref, k_ref, vt_ref, km_ref, o_ref, bias_ref, sa_ref, sb_ref, ma_ref, mb_ref, m8_ref, l8_ref,
                 acc_ref, **kw):
    even = pl.program_id(2) % 2 == 0
    args = (q_ref, k_ref, vt_ref, km_ref, o_ref, bias_ref, m8_ref, l8_ref, acc_ref)
    pl.when(even)(functools.partial(_moba_step, *args, sa_ref, ma_ref, sb_ref, mb_ref, **kw))
    pl.when(jnp.logical_not(even))(functools.partial(_moba_step, *args, sb_ref, mb_ref, sa_ref, ma_ref, **kw))


def _moba_step(q_ref, k_ref, vt_ref, km_ref, o_ref, bias_ref, m8_ref, l8_ref, acc_ref, s_cur, m_cur, s_prev, m_prev,
               *, topk, heads, unroll):
    i = pl.program_id(2)
    blk = q_ref.shape[1]
    nb = km_ref.shape[1]
    trips = (i + unroll - 1) // unroll
    ones = jnp.ones((V7X_SUBLANES, unroll * blk), BF16)

    def fold(x, op):
        return op(x.reshape(blk // V7X_SUBLANES, V7X_SUBLANES, blk), axis=0)

    def score_blocks(it):
        for un in range(unroll):
            n = it * unroll + un
            nk = jnp.minimum(n, nb - 1)
            rows = pl.ds(pl.multiple_of(nk * blk, blk), blk)
            for h in range(heads):
                hs = pl.ds(h * HEAD_DIM, HEAD_DIM)
                s_n = _dot_nt(k_ref[0, rows, hs], q_ref[0, :, hs]) + bias_ref[h, pl.ds(nk, 1), :]
                s_cur[h, n] = s_n
                m8_ref[h] = jnp.maximum(m8_ref[h], fold(s_n, jnp.max))

    def value_blocks(it):
        blocks = [it * unroll + un for un in range(unroll)]
        for h in range(heads):
            p = jnp.concatenate([jnp.exp2(s_prev[h, n] - m_prev[h]).astype(BF16) for n in blocks], axis=0)
            v = jnp.concatenate([vt_ref[0, h, jnp.minimum(n, nb - 1)] for n in blocks], axis=1)
            l8_ref[h] += _dot(ones, p)
            acc_ref[h] += _dot(v, p)

    @pl.when(i < nb)
    def _():
        blk_id = lax.broadcasted_iota(jnp.int32, (nb, blk), 0)
        past = blk_id < i
        for h in range(heads):
            hs = pl.ds(h * HEAD_DIM, HEAD_DIM)
            gate = _dot_nt(km_ref[0, :, hs].astype(BF16), q_ref[0, :, hs])
            gate = jnp.where(past, gate, NEG)
            rank = jnp.zeros((nb, blk), jnp.int32)
            for m in range(nb):
                gm = gate[m:m + 1, :]
                better = (gm > gate) | ((gm == gate) & (blk_id > m))
                rank = rank + better.astype(jnp.int32)
            bias_ref[h] = jnp.where(past & (rank < topk), 0.0, NEG)
            m8_ref[h] = jnp.full(m8_ref.shape[1:], NEG, F32)

    for h in range(heads):
        l8_ref[h] = jnp.zeros(l8_ref.shape[1:], F32)
        acc_ref[h] = jnp.zeros(acc_ref.shape[1:], F32)

    @pl.when(i < nb)
    def _():
        def both(it, carry):
            score_blocks(it)
            value_blocks(it)
            return carry

        lax.fori_loop(0, trips, both, 0)

    @pl.when(i == nb)
    def _():
        def only_values(it, carry):
            value_blocks(it)
            return carry

        lax.fori_loop(0, trips, only_values, 0)

    @pl.when(i < nb)
    def _():
        kpos = lax.broadcasted_iota(jnp.int32, (blk, blk), 0)
        qpos = lax.broadcasted_iota(jnp.int32, (blk, blk), 1)
        own_rows = pl.ds(pl.multiple_of(i * blk, blk), blk)
        for h in range(heads):
            hs = pl.ds(h * HEAD_DIM, HEAD_DIM)
            s = _dot_nt(k_ref[0, own_rows, hs], q_ref[0, :, hs])
            s = jnp.where(kpos <= qpos, s, NEG)
            s_cur[h, i] = s
            for un in range(1, unroll):
                s_cur[h, i + un] = jnp.full((blk, blk), NEG, F32)
            m_cur[h] = jnp.max(jnp.maximum(m8_ref[h], fold(s, jnp.max)), axis=0, keepdims=True)

    @pl.when(i > 0)
    def _():
        for h in range(heads):
            o_ref[0, :, pl.ds(h * HEAD_DIM, HEAD_DIM)] = (acc_ref[h] / l8_ref[h, 0:1, :]).T.astype(BF16)


def _moba(q, k, vt, k_mean):
    bsz, s, d = q.shape
    n_heads = d // HEAD_DIM
    nb = s // MOBA_BLOCK
    heads = _pick(n_heads, 4)
    unroll = 2
    gw = heads * HEAD_DIM
    kern = functools.partial(_moba_kernel, topk=min(MOBA_TOPK, nb), heads=heads, unroll=unroll)
    return pl.pallas_call(
        kern,
        out_shape=jax.ShapeDtypeStruct((bsz, s, d), BF16),
        grid=(bsz, n_heads // heads, nb + 1),
        in_specs=[
            pl.BlockSpec((1, MOBA_BLOCK, gw), lambda b, h, i: (b, jnp.minimum(i, nb - 1), h)),
            pl.BlockSpec((1, s, gw), lambda b, h, i: (b, 0, h)),
            pl.BlockSpec((1, heads, nb, HEAD_DIM, MOBA_BLOCK), lambda b, h, i: (b, h, 0, 0, 0)),
            pl.BlockSpec((1, nb, gw), lambda b, h, i: (b, 0, h)),
        ],
        out_specs=pl.BlockSpec((1, MOBA_BLOCK, gw), lambda b, h, i: (b, jnp.maximum(i - 1, 0), h)),
        scratch_shapes=[pltpu.VMEM((heads, nb, MOBA_BLOCK), F32),
                        pltpu.VMEM((heads, nb + unroll - 1, MOBA_BLOCK, MOBA_BLOCK), F32),
                        pltpu.VMEM((heads, nb + unroll - 1, MOBA_BLOCK, MOBA_BLOCK), F32),
                        pltpu.VMEM((heads, 1, MOBA_BLOCK), F32),
                        pltpu.VMEM((heads, 1, MOBA_BLOCK), F32),
                        pltpu.VMEM((heads, V7X_SUBLANES, MOBA_BLOCK), F32),
                        pltpu.VMEM((heads, V7X_SUBLANES, MOBA_BLOCK), F32),
                        pltpu.VMEM((heads, HEAD_DIM, MOBA_BLOCK), F32)],
        compiler_params=_params(("parallel", "parallel", "arbitrary")),
        name="moba",
    )(q, k, vt, k_mean)


def _oproj_ln_kernel(a_ref, w_ref, x_hbm, g_ref, b_ref, *refs, alpha, n_exp):
    if n_exp:
        wr_ref, o_ref, idx_ref, gate_ref, x_ref, sem = refs
    else:
        o_ref, x_ref, sem = refs
    kk = pl.program_id(1)
    tm = o_ref.shape[0]
    rc = min(tm, 256)
    x_copy = pltpu.make_async_copy(x_hbm.at[pl.ds(pl.program_id(0) * tm, tm)], x_ref, sem)

    @pl.when(kk == 0)
    def _():
        x_copy.start()
        o_ref[...] = jnp.zeros_like(o_ref)

    for c0 in range(0, tm, rc):
        rows = pl.ds(c0, rc)
        o_ref[rows, :] += _dot(a_ref[rows, :], w_ref[...])

    @pl.when(kk == pl.num_programs(1) - 1)
    def _():
        x_copy.wait()
        for c0 in range(0, tm, rc):
            rows = pl.ds(c0, rc)
            y = _layer_norm(alpha * x_ref[rows, :] + o_ref[rows, :], g_ref[...], b_ref[...])
            o_ref[rows, :] = y
            if n_exp:
                logits = _dot(y.astype(BF16), wr_ref[...])
                lane = lax.broadcasted_iota(jnp.int32, logits.shape, 1)
                big = logits.shape[1]
                logits = jnp.where(lane < n_exp, logits, -jnp.inf)
                m1 = jnp.max(logits, axis=1, keepdims=True)
                i1 = jnp.min(jnp.where(logits == m1, lane, big), axis=1, keepdims=True)
                rest = jnp.where(lane == i1, -jnp.inf, logits)
                m2 = jnp.max(rest, axis=1, keepdims=True)
                i2 = jnp.min(jnp.where(rest == m2, lane, big), axis=1, keepdims=True)
                e2 = jnp.exp(m2 - m1)
                den = 1.0 + e2
                idx_ref[rows, :] = jnp.where(lane == 0, i1, jnp.where(lane == 1, i2, 0))
                gate_ref[rows, :] = jnp.where(lane == 0, 1.0 / den, jnp.where(lane == 1, e2 / den, 0.0))


def _oproj_ln(a2, w_o, x2, g, b, alpha, w_router=None):
    t, d = x2.shape
    kdim = a2.shape[1]
    tm = _pick(t, 1024)
    tk = _pick(kdim, 1024)
    n_exp = 0 if w_router is None else w_router.shape[1]
    in_specs = [
        pl.BlockSpec((tm, tk), lambda i, k: (i, k)),
        pl.BlockSpec((tk, d), lambda i, k: (k, 0)),
        pl.BlockSpec(memory_space=pl.ANY),
        pl.BlockSpec((1, d), lambda i, k: (0, 0)),
        pl.BlockSpec((1, d), lambda i, k: (0, 0)),
    ]
    args = [a2, w_o, x2, g.reshape(1, d), b.reshape(1, d)]
    out_shape = [jax.ShapeDtypeStruct((t, d), F32)]
    out_specs = [pl.BlockSpec((tm, d), lambda i, k: (i, 0))]
    if n_exp:
        assert MOE_TOPK == 2 and n_exp <= V7X_LANES
        wr = jnp.pad(w_router, ((0, 0), (0, V7X_LANES - n_exp))).astype(BF16)
        in_specs.append(pl.BlockSpec((d, V7X_LANES), lambda i, k: (0, 0)))
        args.append(wr)
        out_shape += [jax.ShapeDtypeStruct((t, V7X_LANES), jnp.int32), jax.ShapeDtypeStruct((t, V7X_LANES), F32)]
        out_specs += [pl.BlockSpec((tm, V7X_LANES), lambda i, k: (i, 0))] * 2
    kern = functools.partial(_oproj_ln_kernel, alpha=alpha, n_exp=n_exp)
    return pl.pallas_call(
        kern,
        out_shape=out_shape,
        grid=(t // tm, kdim // tk),
        in_specs=in_specs,
        out_specs=out_specs,
        scratch_shapes=[pltpu.VMEM((tm, d), F32), pltpu.SemaphoreType.DMA],
        compiler_params=_params(("parallel", "arbitrary")),
        name="oproj_ln",
    )(*args)


def _route(top_e, tm, n_exp):
    t = top_e.shape[0]
    n_assign = t * MOE_TOPK
    a_exp = top_e.reshape(-1)
    onehot = (a_exp[:, None] == jnp.arange(n_exp, dtype=jnp.int32)[None, :]).astype(jnp.int32)
    csum = jnp.cumsum(onehot, axis=0)
    rank = jnp.sum(csum * onehot, axis=1) - 1
    counts = csum[-1]
    p_counts = (counts + tm - 1) // tm * tm
    p_ends = jnp.cumsum(p_counts)
    p_starts = p_ends - p_counts
    dest = (p_starts[a_exp] + rank).astype(jnp.int32)
    n_blocks = -(-n_assign // tm) + n_exp
    p_rows = n_blocks * tm
    a_tok = jnp.arange(n_assign, dtype=jnp.int32) // MOE_TOPK
    buf_tok = jnp.zeros((p_rows,), jnp.int32).at[dest].set(a_tok)
    blk_start = jnp.arange(n_blocks, dtype=jnp.int32) * tm
    blk_exp = jnp.minimum(jnp.searchsorted(p_ends, blk_start, side="right"), n_exp - 1).astype(jnp.int32)
    blk_rows = jnp.clip(p_starts[blk_exp] + counts[blk_exp] - blk_start, 0, tm).astype(jnp.int32)
    n_used = (p_ends[-1] // tm).astype(jnp.int32).reshape(1)
    return buf_tok, blk_exp, blk_rows, n_used, dest.reshape(t, MOE_TOPK)


def _row_copy(src_hbm, row, dst_vmem, r, sem):
    return pltpu.make_async_copy(src_hbm.at[pl.ds(row, 1)], dst_vmem.at[pl.ds(r, 1)], sem)


def _moe_kernel(exp_ref, rows_ref, used_ref, tok_ref, x_hbm, wg_ref, wu_ref, wd_ref, o_ref,
                xg_ref, xb_ref, wgb_ref, wub_ref, wdb_ref, sem, *, rc):
    i = pl.program_id(0)
    j = pl.program_id(1)
    tm = xg_ref.shape[0]
    n_used = used_ref[0]
    active = i < n_used

    def start_gather(tile):
        for c0 in range(0, tm, rc):
            @pl.when(c0 < rows_ref[tile])
            def _(c0=c0):
                def start(r, c):
                    _row_copy(x_hbm, tok_ref[tile * tm + c0 + r], xg_ref, c0 + r, sem).start()
                    return c

                lax.fori_loop(0, rc, start, 0, unroll=8)

    @pl.when(jnp.logical_and(i == 0, j == 0))
    def _():
        start_gather(0)

    @pl.when(jnp.logical_and(j == 1, i + 1 < n_used))
    def _():
        start_gather(i + 1)

    @pl.when(j == 0)
    def _():
        o_ref[...] = jnp.zeros_like(o_ref)

    @pl.when(jnp.logical_and(active, j == 0))
    def _():
        for c0 in range(0, tm, rc):
            @pl.when(c0 < rows_ref[i])
            def _(c0=c0):
                rows = pl.ds(c0, rc)
                pltpu.make_async_copy(x_hbm.at[rows], xg_ref.at[rows], sem).wait()
        for c0 in range(0, tm, rc):
            @pl.when(c0 < rows_ref[i])
            def _(c0=c0):
                rows = pl.ds(c0, rc)
                xb_ref[rows, :] = xg_ref[rows, :].astype(BF16)

    @pl.when(active)
    def _():
        _ffn_accumulate(xb_ref, wg_ref.at[0], wu_ref.at[0], wd_ref.at[0], wgb_ref, wub_ref, wdb_ref, o_ref,
                        _pick(tm, 512), n_rows=rows_ref[i], rc_part=rc)


def _moe(x2, buf_tok, blk_exp, blk_rows, n_used, w_gate, w_up, w_down, tm):
    t, d = x2.shape
    n_exp, _, f = w_gate.shape
    tf = _pick(f, 256)
    rc = _pick(tm, 256)
    n_blocks = blk_exp.shape[0]
    nj = f // tf
    assert nj >= 2

    def w_idx(i, j, exp_ref, rows_ref, used_ref, tok_ref):
        last = used_ref[0] - 1
        return exp_ref[jnp.minimum(i, last)], jnp.where(i <= last, j, nj - 1)

    def wgu_map(i, j, *s):
        e, jj = w_idx(i, j, *s)
        return e, 0, jj

    def wd_map(i, j, *s):
        e, jj = w_idx(i, j, *s)
        return e, jj, 0

    grid_spec = pltpu.PrefetchScalarGridSpec(
        num_scalar_prefetch=4,
        grid=(n_blocks, nj),
        in_specs=[
            pl.BlockSpec(memory_space=pl.ANY),
            pl.BlockSpec((1, d, tf), wgu_map),
            pl.BlockSpec((1, d, tf), wgu_map),
            pl.BlockSpec((1, tf, d), wd_map),
        ],
        out_specs=pl.BlockSpec((tm, d), lambda i, j, *s: (i, 0)),
        scratch_shapes=[pltpu.VMEM((tm, d), F32), pltpu.VMEM((tm, d), BF16), pltpu.VMEM((d, tf), BF16),
                        pltpu.VMEM((d, tf), BF16), pltpu.VMEM((tf, d), BF16), pltpu.SemaphoreType.DMA],
    )
    return pl.pallas_call(
        functools.partial(_moe_kernel, rc=rc),
        out_shape=jax.ShapeDtypeStruct((n_blocks * tm, d), F32),
        grid_spec=grid_spec,
        compiler_params=_params(("arbitrary", "arbitrary")),
        name="moe",
    )(blk_exp, blk_rows, n_used, buf_tok, x2, w_gate, w_up, w_down)


def _combine_ln_kernel(p0_ref, p1_ref, x_ref, gate_ref, y_hbm, g_ref, b_ref, o_ref, ybuf_ref, sems, *, alpha):
    i = pl.program_id(0)
    tm = x_ref.shape[0]
    slot = i % 2

    def start_gather(tile, sl):
        def start(r, c):
            for k, p_ref in enumerate((p0_ref, p1_ref)):
                _row_copy(y_hbm, p_ref[tile * tm + r], ybuf_ref.at[sl, k], r, sems.at[sl, k]).start()
            return c

        lax.fori_loop(0, tm, start, 0, unroll=8)

    @pl.when(i == 0)
    def _():
        start_gather(0, 0)

    @pl.when(i + 1 < pl.num_programs(0))
    def _():
        start_gather(i + 1, 1 - slot)

    for k in range(MOE_TOPK):
        pltpu.make_async_copy(y_hbm.at[pl.ds(0, tm)], ybuf_ref.at[slot, k], sems.at[slot, k]).wait()
    rc = _pick(tm, 256)
    for c0 in range(0, tm, rc):
        rows = pl.ds(c0, rc)
        gates = gate_ref[rows, :]
        y = ybuf_ref[slot, 0, rows, :] * gates[:, 0:1] + ybuf_ref[slot, 1, rows, :] * gates[:, 1:2]
        o_ref[rows, :] = _layer_norm(alpha * x_ref[rows, :] + y, g_ref[...], b_ref[...])


def _combine_ln(x2, gates, y_sorted, pos, g, b, alpha):
    t, d = x2.shape
    tm = _pick(t, 256)
    grid_spec = pltpu.PrefetchScalarGridSpec(
        num_scalar_prefetch=2,
        grid=(t // tm,),
        in_specs=[
            pl.BlockSpec((tm, d), lambda i, *s: (i, 0)),
            pl.BlockSpec((tm, gates.shape[1]), lambda i, *s: (i, 0)),
            pl.BlockSpec(memory_space=pl.ANY),
            pl.BlockSpec((1, d), lambda i, *s: (0, 0)),
            pl.BlockSpec((1, d), lambda i, *s: (0, 0)),
        ],
        out_specs=pl.BlockSpec((tm, d), lambda i, *s: (i, 0)),
        scratch_shapes=[pltpu.VMEM((2, MOE_TOPK, tm, d), F32), pltpu.SemaphoreType.DMA((2, MOE_TOPK))],
    )
    kern = functools.partial(_combine_ln_kernel, alpha=alpha)
    return pl.pallas_call(
        kern,
        out_shape=jax.ShapeDtypeStruct((t, d), F32),
        grid_spec=grid_spec,
        compiler_params=_params(("arbitrary",)),
        name="combine_ln",
    )(pos[:, 0], pos[:, 1], x2, gates, y_sorted, g.reshape(1, d), b.reshape(1, d))


def _rope_tables(s):
    inv = 1.0 / (ROPE_THETA ** (jnp.arange(0, HEAD_DIM, 2, dtype=F32) / HEAD_DIM))
    ang = jnp.arange(s, dtype=F32)[:, None] * inv[None, :]
    ang = jnp.concatenate([ang, ang], axis=-1)
    sign = jnp.where(jnp.arange(HEAD_DIM) < HEAD_DIM // 2, -1.0, 1.0).astype(F32)
    return jnp.cos(ang), jnp.sin(ang) * sign[None, :]


def kernel(x, pool_w, pool_scale, w_kv, moba_wq, moba_wo, ffn_w_gate, ffn_w_up, ffn_w_down,
           moe_router, moe_w_gate, moe_w_up, moe_w_down, ln_mix_g, ln_mix_b, ln_ffn_g, ln_ffn_b):
    bsz, s, d = x.shape
    t = bsz * s
    depth = ln_mix_g.shape[0]
    n_a = pool_w.shape[0]
    alpha = (2.0 * depth) ** 0.25
    cos, sin_signed = _rope_tables(s)
    nb = s // MOBA_BLOCK
    assert s % MOBA_BLOCK == 0 and d % HEAD_DIM == 0
    kv = None
    x2 = x.reshape(t, d)
    xb = None
    for l in range(depth):
        if l < n_a:
            x3, xb3 = _pool_ln(x2.reshape(bsz, s, d), pool_w[l], pool_scale[l], ln_mix_g[l], ln_mix_b[l], alpha)
            x2, xb = x3.reshape(t, d), xb3.reshape(t, d)
            router = None
        else:
            bl = l - n_a
            k, vt, k_mean = kv
            xb = _to_bf16(x2) if xb is None else xb
            q = _proj_rope(xb, _to_bf16(moba_wq[bl]), 0, d, cos, sin_signed, s, with_mean=False,
                           out_scale=MOBA_Q_SCALE)[0]
            att = _moba(q.reshape(bsz, s, d), k.reshape(bsz, s, d), vt, k_mean)
            router = moe_router[l // 2] if l % 2 == 1 else None
            res = _oproj_ln(att.reshape(t, d), _to_bf16(moba_wo[bl]), x2, ln_mix_g[l], ln_mix_b[l], alpha, router)
            x2, xb = res[0], None
        jf = l // 2
        if l % 2 == 0:
            xb = _to_bf16(x2) if xb is None else xb
            x2, xb = _swiglu_ln(x2, xb, ffn_w_gate[jf], ffn_w_up[jf], ffn_w_down[jf], ln_ffn_g[l], ln_ffn_b[l],
                                alpha)
        else:
            n_exp = moe_router.shape[2]
            if router is None:
                raise NotImplementedError("MoE after a pooling mixer is not supported")
            tm = _pick(t * MOE_TOPK, 1024)
            buf_tok, blk_exp, blk_rows, n_used, pos = _route(res[1][:, :MOE_TOPK], tm, n_exp)
            y_sorted = _moe(x2, buf_tok, blk_exp, blk_rows, n_used,
                            moe_w_gate[jf], moe_w_up[jf], moe_w_down[jf], tm)
            x2, xb = _combine_ln(x2, res[2], y_sorted, pos, ln_ffn_g[l], ln_ffn_b[l], alpha), None
        if l == n_a - 1:
            xb = _to_bf16(x2) if xb is None else xb
            wkvb = _to_bf16(w_kv)
            k, k_mean = _proj_rope(xb, wkvb, 0, d, cos, sin_signed, s, with_mean=True)
            vt = _proj_vt(xb, wkvb, d, d, bsz, s)
            kv = (k, vt, k_mean.reshape(bsz, nb, d))
    return x2.reshape(bsz, s, d)
```

```python
import functools

import jax
import jax.numpy as jnp
from jax import lax
from jax.experimental import pallas as pl
from jax.experimental.pallas import tpu as pltpu

HEAD_DIM = 128
MOBA_BLOCK = 256
MOBA_TOPK = 3
ROPE_THETA = 10000.0
POOL_WINDOWS = (2, 4, 8, 16)
MOE_TOPK = 2
LN_EPS = 1e-5
NEG = -1e30
LOG2_E = 1.4426950408889634
MOBA_Q_SCALE = HEAD_DIM ** -0.5 * LOG2_E

V7X_LANES = 128
V7X_SUBLANES = 8
V7X_VMEM_LIMIT_BYTES = 56 * 1024 * 1024

F32 = jnp.float32
BF16 = jnp.bfloat16


def _dot(a, b):
    return jnp.dot(a, b, preferred_element_type=F32)


def _dot_nt(a, b):
    return lax.dot_general(a, b, (((1,), (1,)), ((), ())), preferred_element_type=F32)


def _layer_norm(y, g, b):
    mu = jnp.mean(y, axis=-1, keepdims=True)
    d = y - mu
    var = jnp.mean(d * d, axis=-1, keepdims=True)
    return d * lax.rsqrt(var + LN_EPS) * g + b


def _params(semantics):
    return pltpu.CompilerParams(dimension_semantics=semantics, vmem_limit_bytes=V7X_VMEM_LIMIT_BYTES)


def _pick(dim, pref):
    t = min(dim, pref)
    while dim % t:
        t //= 2
    return t


def _ln_rows(x_ref, o_ref, g_ref, b_ref, alpha, rc):
    tm = o_ref.shape[0]
    for c0 in range(0, tm, rc):
        rows = pl.ds(c0, min(rc, tm - c0))
        o_ref[rows, :] = _layer_norm(alpha * x_ref[rows, :] + o_ref[rows, :], g_ref[...], b_ref[...])


def _pool_ln_kernel(xh_ref, x_ref, w_ref, sc_ref, g_ref, b_ref, o_ref, ob_ref, *, alpha, windows, halo):
    i = pl.program_id(1)
    ts, d = x_ref.shape[1], x_ref.shape[2]
    c = d // len(windows)
    t_pos = i * ts + lax.broadcasted_iota(jnp.int32, (ts, 1), 0)
    for g, w in enumerate(windows):
        cols = pl.ds(g * c, c)
        x = x_ref[0, :, cols]
        prev = jnp.where(i > 0, xh_ref[0, :, cols], 0.0)
        cur = jnp.concatenate([prev, x], axis=0)
        width = 1
        while width < w:
            cur = cur + pltpu.roll(cur, width, axis=0)
            width *= 2
        cnt = jnp.minimum(t_pos + 1, w).astype(F32)
        diff = cur[halo:, :] / cnt - x
        y = _dot(diff.astype(BF16), w_ref[g].astype(BF16))
        o_ref[0, :, cols] = y * sc_ref[:, cols]
    _ln_rows(x_ref.at[0], o_ref.at[0], g_ref, b_ref, alpha, 256)
    ob_ref[0] = o_ref[0].astype(BF16)


def _pool_ln(x, w_groups, scale, g, b, alpha):
    bsz, s, d = x.shape
    windows = POOL_WINDOWS
    assert len(windows) == w_groups.shape[0] and d % len(windows) == 0
    assert all(w & (w - 1) == 0 for w in windows) and list(windows) == sorted(windows)
    halo = -(-max(windows) // V7X_SUBLANES) * V7X_SUBLANES
    ts = _pick(s, 512)
    assert ts % halo == 0
    r = ts // halo
    kern = functools.partial(_pool_ln_kernel, alpha=alpha, windows=windows, halo=halo)
    return pl.pallas_call(
        kern,
        out_shape=[jax.ShapeDtypeStruct(x.shape, F32), jax.ShapeDtypeStruct(x.shape, BF16)],
        grid=(bsz, s // ts),
        in_specs=[
            pl.BlockSpec((1, halo, d), lambda bi, i: (bi, jnp.maximum(i * r - 1, 0), 0)),
            pl.BlockSpec((1, ts, d), lambda bi, i: (bi, i, 0)),
            pl.BlockSpec(w_groups.shape, lambda bi, i: (0, 0, 0)),
            pl.BlockSpec((1, d), lambda bi, i: (0, 0)),
            pl.BlockSpec((1, d), lambda bi, i: (0, 0)),
            pl.BlockSpec((1, d), lambda bi, i: (0, 0)),
        ],
        out_specs=[pl.BlockSpec((1, ts, d), lambda bi, i: (bi, i, 0))] * 2,
        compiler_params=_params(("parallel", "parallel")),
        name="pool_ln",
    )(x, x, w_groups, scale.reshape(1, d), g.reshape(1, d), b.reshape(1, d))


def _ffn_accumulate(xb_ref, wg_ref, wu_ref, wd_ref, wgb_ref, wub_ref, wdb_ref, o_ref, rc, first, n_rows=None,
                    rc_part=None):
    wgb_ref[...] = wg_ref[...].astype(BF16)
    wub_ref[...] = wu_ref[...].astype(BF16)
    wdb_ref[...] = wd_ref[...].astype(BF16)
    tm = xb_ref.shape[0]

    def chunk(c0, size, assign):
        rows = pl.ds(c0, size)
        xb = xb_ref[rows, :]
        a = _dot(xb, wgb_ref[...])
        u = _dot(xb, wub_ref[...])
        h = (a * jax.nn.sigmoid(a)) * u
        part = _dot(h.astype(BF16), wdb_ref[...])
        if assign:
            o_ref[rows, :] = part
        else:
            o_ref[rows, :] += part

    def full(assign):
        for c0 in range(0, tm, rc):
            chunk(c0, rc, assign)

    is_full = True if n_rows is None else n_rows == tm
    pl.when(jnp.logical_and(is_full, first))(functools.partial(full, True))
    pl.when(jnp.logical_and(is_full, jnp.logical_not(first)))(functools.partial(full, False))
    if n_rows is None:
        return
    n_chunks = (n_rows + rc_part - 1) // rc_part
    for live in range(1, tm // rc_part + 1):
        @pl.when(jnp.logical_and(n_rows < tm, n_chunks == live))
        def _(live=live):
            for c in range(live):
                chunk(c * rc_part, rc_part, False)


def _swiglu_ln_kernel(xb_ref, x_hbm, wg_ref, wu_ref, wd_ref, g_ref, b_ref, o_ref, ob_ref, wgb_ref, wub_ref, wdb_ref,
                      xr_ref, sems, *, alpha):
    i = pl.program_id(0)
    j = pl.program_id(1)
    tm = o_ref.shape[0]
    rl = xr_ref.shape[1]

    def x_copy(c):
        return pltpu.make_async_copy(x_hbm.at[pl.ds(i * tm + c * rl, rl)], xr_ref.at[c % 2], sems.at[c % 2])

    @pl.when(j == 0)
    def _():
        x_copy(0).start()

    _ffn_accumulate(xb_ref, wg_ref, wu_ref, wd_ref, wgb_ref, wub_ref, wdb_ref, o_ref, _pick(tm, 512), j == 0)

    @pl.when(j == pl.num_programs(1) - 1)
    def _():
        n_chunks = tm // rl
        for c in range(n_chunks):
            if c + 1 < n_chunks:
                x_copy(c + 1).start()
            x_copy(c).wait()
            rows = pl.ds(c * rl, rl)
            y = _layer_norm(alpha * xr_ref[c % 2] + o_ref[rows, :], g_ref[...], b_ref[...])
            o_ref[rows, :] = y
            ob_ref[rows, :] = y.astype(BF16)


def _swiglu_ln(x2, xb, w_gate, w_up, w_down, g, b, alpha):
    t, d = x2.shape
    f = w_gate.shape[1]
    tm = _pick(t, 1024)
    tf = _pick(f, 256)
    rl = _pick(tm, 128)
    kern = functools.partial(_swiglu_ln_kernel, alpha=alpha)
    return pl.pallas_call(
        kern,
        out_shape=[jax.ShapeDtypeStruct((t, d), F32), jax.ShapeDtypeStruct((t, d), BF16)],
        grid=(t // tm, f // tf),
        in_specs=[
            pl.BlockSpec((tm, d), lambda i, j: (i, 0)),
            pl.BlockSpec(memory_space=pl.ANY),
            pl.BlockSpec((d, tf), lambda i, j: (0, j)),
            pl.BlockSpec((d, tf), lambda i, j: (0, j)),
            pl.BlockSpec((tf, d), lambda i, j: (j, 0)),
            pl.BlockSpec((1, d), lambda i, j: (0, 0)),
            pl.BlockSpec((1, d), lambda i, j: (0, 0)),
        ],
        out_specs=[pl.BlockSpec((tm, d), lambda i, j: (i, 0)), pl.BlockSpec((tm, d), lambda i, j: (i, 0))],
        scratch_shapes=[pltpu.VMEM((d, tf), BF16), pltpu.VMEM((d, tf), BF16), pltpu.VMEM((tf, d), BF16),
                        pltpu.VMEM((2, rl, d), F32), pltpu.SemaphoreType.DMA((2,))],
        compiler_params=_params(("parallel", "arbitrary")),
        name="swiglu_ln",
    )(xb, x2, w_gate, w_up, w_down, g.reshape(1, d), b.reshape(1, d))


def _to_bf16_kernel(w_ref, o_ref):
    o_ref[...] = w_ref[...].astype(BF16)


def _to_bf16(w):
    r, c = w.shape
    tr = _pick(r, 512)
    return pl.pallas_call(
        _to_bf16_kernel,
        out_shape=jax.ShapeDtypeStruct((r, c), BF16),
        grid=(r // tr,),
        in_specs=[pl.BlockSpec((tr, c), lambda i: (i, 0))],
        out_specs=pl.BlockSpec((tr, c), lambda i: (i, 0)),
        compiler_params=_params(("parallel",)),
        name="to_bf16",
    )(w)


def _proj_rope_kernel(x_ref, w_ref, cos_ref, sin_ref, o_ref, *km_ref, blk, out_scale):
    tm, tn = o_ref.shape
    for c0 in range(0, tm, blk):
        rows = pl.ds(c0, blk)
        y = _dot(x_ref[rows, :], w_ref[...])
        cos = cos_ref[rows, :]
        sin = sin_ref[rows, :]
        if out_scale != 1.0:
            cos, sin = cos * out_scale, sin * out_scale
        heads = []
        for h in range(tn // HEAD_DIM):
            th = y[:, h * HEAD_DIM:(h + 1) * HEAD_DIM]
            heads.append(th * cos + pltpu.roll(th, HEAD_DIM // 2, axis=1) * sin)
        yr = heads[0] if len(heads) == 1 else jnp.concatenate(heads, axis=1)
        o_ref[rows, :] = yr.astype(BF16)
        if km_ref:
            km_ref[0][0, pl.ds(c0 // blk, 1), :] = jnp.sum(yr, axis=0, keepdims=True) * (1.0 / blk)


def _proj_rope(xb, wb, col0, n_out, cos, sin_signed, s, with_mean, out_scale=1.0):
    t, d = xb.shape
    tm = _pick(s, 1024)
    tn = _pick(n_out, 1024)
    assert tm % MOBA_BLOCK == 0 and tn % HEAD_DIM == 0 and col0 % tn == 0
    s_tiles = s // tm
    jb = col0 // tn
    out_shape = [jax.ShapeDtypeStruct((t, n_out), BF16)]
    out_specs = [pl.BlockSpec((tm, tn), lambda i, j: (i, j))]
    if with_mean:
        out_shape.append(jax.ShapeDtypeStruct((t // tm, tm // MOBA_BLOCK, n_out), F32))
        out_specs.append(pl.BlockSpec((1, tm // MOBA_BLOCK, tn), lambda i, j: (i, 0, j)))
    assert not (with_mean and out_scale != 1.0)
    kern = functools.partial(_proj_rope_kernel, blk=MOBA_BLOCK, out_scale=out_scale)
    return pl.pallas_call(
        kern,
        out_shape=out_shape,
        grid=(t // tm, n_out // tn),
        in_specs=[
            pl.BlockSpec((tm, d), lambda i, j: (i, 0)),
            pl.BlockSpec((d, tn), lambda i, j: (0, j + jb)),
            pl.BlockSpec((tm, HEAD_DIM), lambda i, j: (i % s_tiles, 0)),
            pl.BlockSpec((tm, HEAD_DIM), lambda i, j: (i % s_tiles, 0)),
        ],
        out_specs=out_specs,
        compiler_params=_params(("parallel", "arbitrary")),
        name="proj_rope_k" if with_mean else "proj_rope_q",
    )(xb, wb, cos, sin_signed)


def _proj_vt_kernel(x_ref, w_ref, o_ref, *, blk):
    tm = x_ref.shape[0]
    tn = w_ref.shape[1]
    for c in range(tm // blk):
        yt = _dot(x_ref[pl.ds(c * blk, blk), :], w_ref[...]).T
        for h in range(tn // HEAD_DIM):
            o_ref[0, h, c] = yt[h * HEAD_DIM:(h + 1) * HEAD_DIM, :].astype(BF16)


def _proj_vt(xb, wb, col0, n_out, bsz, s):
    t, d = xb.shape
    tm = _pick(s, 1024)
    tn = _pick(n_out, 1024)
    assert tm % MOBA_BLOCK == 0 and tn % HEAD_DIM == 0 and col0 % tn == 0
    s_tiles = s // tm
    jb = col0 // tn
    hpt = tn // HEAD_DIM
    nbt = tm // MOBA_BLOCK
    kern = functools.partial(_proj_vt_kernel, blk=MOBA_BLOCK)
    return pl.pallas_call(
        kern,
        out_shape=jax.ShapeDtypeStruct((bsz, n_out // HEAD_DIM, s // MOBA_BLOCK, HEAD_DIM, MOBA_BLOCK), BF16),
        grid=(t // tm, n_out // tn),
        in_specs=[
            pl.BlockSpec((tm, d), lambda i, j: (i, 0)),
            pl.BlockSpec((d, tn), lambda i, j: (0, j + jb)),
        ],
        out_specs=pl.BlockSpec((1, hpt, nbt, HEAD_DIM, MOBA_BLOCK),
                               lambda i, j: (i // s_tiles, j, i % s_tiles, 0, 0)),
        compiler_params=_params(("parallel", "arbitrary")),
        name="proj_vt",
    )(xb, wb)


def _moba_kernel(q_ref, k_ref, vt_ref, km_ref, o_ref, bias_ref, sa_ref, sb_ref, ma_ref, mb_ref, m8_ref, l8_ref,
                 acc_ref, **kw):
    even = pl.program_id(2) % 2 == 0
    args = (q_ref, k_ref, vt_ref, km_ref, o_ref, bias_ref, m8_ref, l8_ref, acc_ref)
    pl.when(even)(functools.partial(_moba_step, *args, sa_ref, ma_ref, sb_ref, mb_ref, **kw))
    pl.when(jnp.logical_not(even))(functools.partial(_moba_step, *args, sb_ref, mb_ref, sa_ref, ma_ref, **kw))


def _moba_step(q_ref, k_ref, vt_ref, km_ref, o_ref, bias_ref, m8_ref, l8_ref, acc_ref, s_cur, m_cur, s_prev, m_prev,
               *, topk, heads, unroll):
    i = pl.program_id(2)
    blk = q_ref.shape[1]
    nb = km_ref.shape[1]
    trips = (i + unroll - 1) // unroll
    ones = jnp.ones((V7X_SUBLANES, unroll * blk), BF16)

    def fold(x, op):
        return op(x.reshape(blk // V7X_SUBLANES, V7X_SUBLANES, blk), axis=0)

    def score_blocks(it):
        for un in range(unroll):
            n = it * unroll + un
            nk = jnp.minimum(n, nb - 1)
            rows = pl.ds(pl.multiple_of(nk * blk, blk), blk)
            for h in range(heads):
                hs = pl.ds(h * HEAD_DIM, HEAD_DIM)
                s_n = _dot_nt(k_ref[0, rows, hs], q_ref[0, :, hs]) + bias_ref[h, pl.ds(nk, 1), :]
                s_cur[h, n] = s_n
                m8_ref[h] = jnp.maximum(m8_ref[h], fold(s_n, jnp.max))

    def value_blocks(it):
        blocks = [it * unroll + un for un in range(unroll)]
        for h in range(heads):
            p = jnp.concatenate([jnp.exp2(s_prev[h, n] - m_prev[h]).astype(BF16) for n in blocks], axis=0)
            v = jnp.concatenate([vt_ref[0, h, jnp.minimum(n, nb - 1)] for n in blocks], axis=1)
            l8_ref[h] += _dot(ones, p)
            acc_ref[h] += _dot(v, p)

    @pl.when(i < nb)
    def _():
        blk_id = lax.broadcasted_iota(jnp.int32, (nb, blk), 0)
        past = blk_id < i
        for h in range(heads):
            hs = pl.ds(h * HEAD_DIM, HEAD_DIM)
            gate = _dot_nt(km_ref[0, :, hs].astype(BF16), q_ref[0, :, hs])
            gate = jnp.where(past, gate, NEG)
            rank = jnp.zeros((nb, blk), jnp.int32)
            for m in range(nb):
                gm = gate[m:m + 1, :]
                better = (gm > gate) | ((gm == gate) & (blk_id > m))
                rank = rank + better.astype(jnp.int32)
            bias_ref[h] = jnp.where(past & (rank < topk), 0.0, NEG)
            m8_ref[h] = jnp.full(m8_ref.shape[1:], NEG, F32)

    for h in range(heads):
        l8_ref[h] = jnp.zeros(l8_ref.shape[1:], F32)
        acc_ref[h] = jnp.zeros(acc_ref.shape[1:], F32)

    @pl.when(i < nb)
    def _():
        def both(it, carry):
            score_blocks(it)
            value_blocks(it)
            return carry

        lax.fori_loop(0, trips, both, 0)

    @pl.when(i == nb)
    def _():
        def only_values(it, carry):
            value_blocks(it)
            return carry

        lax.fori_loop(0, trips, only_values, 0)

    @pl.when(i < nb)
    def _():
        kpos = lax.broadcasted_iota(jnp.int32, (blk, blk), 0)
        qpos = lax.broadcasted_iota(jnp.int32, (blk, blk), 1)
        own_rows = pl.ds(pl.multiple_of(i * blk, blk), blk)
        for h in range(heads):
            hs = pl.ds(h * HEAD_DIM, HEAD_DIM)
            s = _dot_nt(k_ref[0, own_rows, hs], q_ref[0, :, hs])
            s = jnp.where(kpos <= qpos, s, NEG)
            s_cur[h, i] = s
            for un in range(1, unroll):
                s_cur[h, i + un] = jnp.full((blk, blk), NEG, F32)
            m_cur[h] = jnp.max(jnp.maximum(m8_ref[h], fold(s, jnp.max)), axis=0, keepdims=True)

    @pl.when(i > 0)
    def _():
        for h in range(heads):
            o_ref[0, :, pl.ds(h * HEAD_DIM, HEAD_DIM)] = (acc_ref[h] / l8_ref[h, 0:1, :]).T.astype(BF16)


def _moba(q, k, vt, k_mean):
    bsz, s, d = q.shape
    n_heads = d // HEAD_DIM
    nb = s // MOBA_BLOCK
    heads = _pick(n_heads, 4)
    unroll = 2
    gw = heads * HEAD_DIM
    kern = functools.partial(_moba_kernel, topk=min(MOBA_TOPK, nb), heads=heads, unroll=unroll)
    return pl.pallas_call(
        kern,
        out_shape=jax.ShapeDtypeStruct((bsz, s, d), BF16),
        grid=(bsz, n_heads // heads, nb + 1),
        in_specs=[
            pl.BlockSpec((1, MOBA_BLOCK, gw), lambda b, h, i: (b, jnp.minimum(i, nb - 1), h)),
            pl.BlockSpec((1, s, gw), lambda b, h, i: (b, 0, h)),
            pl.BlockSpec((1, heads, nb, HEAD_DIM, MOBA_BLOCK), lambda b, h, i: (b, h, 0, 0, 0)),
            pl.BlockSpec((1, nb, gw), lambda b, h, i: (b, 0, h)),
        ],
        out_specs=pl.BlockSpec((1, MOBA_BLOCK, gw), lambda b, h, i: (b, jnp.maximum(i - 1, 0), h)),
        scratch_shapes=[pltpu.VMEM((heads, nb, MOBA_BLOCK), F32),
                        pltpu.VMEM((heads, nb + unroll - 1, MOBA_BLOCK, MOBA_BLOCK), F32),
                        pltpu.VMEM((heads, nb + unroll - 1, MOBA_BLOCK, MOBA_BLOCK), F32),
                        pltpu.VMEM((heads, 1, MOBA_BLOCK), F32),
                        pltpu.VMEM((heads, 1, MOBA_BLOCK), F32),
                        pltpu.VMEM((heads, V7X_SUBLANES, MOBA_BLOCK), F32),
                        pltpu.VMEM((heads, V7X_SUBLANES, MOBA_BLOCK), F32),
                        pltpu.VMEM((heads, HEAD_DIM, MOBA_BLOCK), F32)],
        compiler_params=_params(("parallel", "parallel", "arbitrary")),
        name="moba",
    )(q, k, vt, k_mean)


def _oproj_ln_kernel(a_ref, w_ref, x_hbm, g_ref, b_ref, *refs, alpha, n_exp):
    if n_exp:
        wr_ref, o_ref, idx_ref, gate_ref, x_ref, sem = refs
    else:
        o_ref, x_ref, sem = refs
    kk = pl.program_id(1)
    tm = o_ref.shape[0]
    rc = min(tm, 256)
    x_copy = pltpu.make_async_copy(x_hbm.at[pl.ds(pl.program_id(0) * tm, tm)], x_ref, sem)

    @pl.when(kk == 0)
    def _():
        x_copy.start()
        for c0 in range(0, tm, rc):
            rows = pl.ds(c0, rc)
            o_ref[rows, :] = _dot(a_ref[rows, :], w_ref[...])

    @pl.when(kk > 0)
    def _():
        for c0 in range(0, tm, rc):
            rows = pl.ds(c0, rc)
            o_ref[rows, :] += _dot(a_ref[rows, :], w_ref[...])

    @pl.when(kk == pl.num_programs(1) - 1)
    def _():
        x_copy.wait()
        for c0 in range(0, tm, rc):
            rows = pl.ds(c0, rc)
            y = _layer_norm(alpha * x_ref[rows, :] + o_ref[rows, :], g_ref[...], b_ref[...])
            o_ref[rows, :] = y
            if n_exp:
                logits = _dot(y.astype(BF16), wr_ref[...])
                lane = lax.broadcasted_iota(jnp.int32, logits.shape, 1)
                big = logits.shape[1]
                logits = jnp.where(lane < n_exp, logits, -jnp.inf)
                m1 = jnp.max(logits, axis=1, keepdims=True)
                i1 = jnp.min(jnp.where(logits == m1, lane, big), axis=1, keepdims=True)
                rest = jnp.where(lane == i1, -jnp.inf, logits)
                m2 = jnp.max(rest, axis=1, keepdims=True)
                i2 = jnp.min(jnp.where(rest == m2, lane, big), axis=1, keepdims=True)
                e2 = jnp.exp(m2 - m1)
                den = 1.0 + e2
                idx_ref[rows, :] = jnp.where(lane == 0, i1, jnp.where(lane == 1, i2, 0))
                gate_ref[rows, :] = jnp.where(lane == 0, 1.0 / den, jnp.where(lane == 1, e2 / den, 0.0))


def _oproj_ln(a2, w_o, x2, g, b, alpha, w_router=None):
    t, d = x2.shape
    kdim = a2.shape[1]
    tm = _pick(t, 1024)
    tk = _pick(kdim, 1024)
    n_exp = 0 if w_router is None else w_router.shape[1]
    in_specs = [
        pl.BlockSpec((tm, tk), lambda i, k: (i, k)),
        pl.BlockSpec((tk, d), lambda i, k: (k, 0)),
        pl.BlockSpec(memory_space=pl.ANY),
        pl.BlockSpec((1, d), lambda i, k: (0, 0)),
        pl.BlockSpec((1, d), lambda i, k: (0, 0)),
    ]
    args = [a2, w_o, x2, g.reshape(1, d), b.reshape(1, d)]
    out_shape = [jax.ShapeDtypeStruct((t, d), F32)]
    out_specs = [pl.BlockSpec((tm, d), lambda i, k: (i, 0))]
    if n_exp:
        assert MOE_TOPK == 2 and n_exp <= V7X_LANES
        wr = jnp.pad(w_router, ((0, 0), (0, V7X_LANES - n_exp))).astype(BF16)
        in_specs.append(pl.BlockSpec((d, V7X_LANES), lambda i, k: (0, 0)))
        args.append(wr)
        out_shape += [jax.ShapeDtypeStruct((t, V7X_LANES), jnp.int32), jax.ShapeDtypeStruct((t, V7X_LANES), F32)]
        out_specs += [pl.BlockSpec((tm, V7X_LANES), lambda i, k: (i, 0))] * 2
    kern = functools.partial(_oproj_ln_kernel, alpha=alpha, n_exp=n_exp)
    return pl.pallas_call(
        kern,
        out_shape=out_shape,
        grid=(t // tm, kdim // tk),
        in_specs=in_specs,
        out_specs=out_specs,
        scratch_shapes=[pltpu.VMEM((tm, d), F32), pltpu.SemaphoreType.DMA],
        compiler_params=_params(("parallel", "arbitrary")),
        name="oproj_ln",
    )(*args)


def _route(top_e, tm, n_exp):
    t = top_e.shape[0]
    n_assign = t * MOE_TOPK
    a_exp = top_e.reshape(-1)
    onehot = (a_exp[:, None] == jnp.arange(n_exp, dtype=jnp.int32)[None, :]).astype(jnp.int32)
    csum = jnp.cumsum(onehot, axis=0)
    rank = jnp.sum(csum * onehot, axis=1) - 1
    counts = csum[-1]
    p_counts = (counts + tm - 1) // tm * tm
    p_ends = jnp.cumsum(p_counts)
    p_starts = p_ends - p_counts
    dest = (p_starts[a_exp] + rank).astype(jnp.int32)
    n_blocks = -(-n_assign // tm) + n_exp - 1
    p_rows = n_blocks * tm
    a_tok = jnp.arange(n_assign, dtype=jnp.int32) // MOE_TOPK
    buf_tok = jnp.zeros((p_rows,), jnp.int32).at[dest].set(a_tok)
    blk_start = jnp.arange(n_blocks, dtype=jnp.int32) * tm
    blk_exp = jnp.minimum(jnp.searchsorted(p_ends, blk_start, side="right"), n_exp - 1).astype(jnp.int32)
    blk_rows = jnp.clip(p_starts[blk_exp] + counts[blk_exp] - blk_start, 0, tm).astype(jnp.int32)
    n_used = (p_ends[-1] // tm).astype(jnp.int32).reshape(1)
    return buf_tok, blk_exp, blk_rows, n_used, dest.reshape(t, MOE_TOPK)


def _row_copy(src_hbm, row, dst_vmem, r, sem):
    return pltpu.make_async_copy(src_hbm.at[pl.ds(row, 1)], dst_vmem.at[pl.ds(r, 1)], sem)


def _moe_kernel(exp_ref, rows_ref, used_ref, tok_ref, x_hbm, wg_ref, wu_ref, wd_ref, o_ref,
                xg_ref, xb_ref, wgb_ref, wub_ref, wdb_ref, sem, *, rc):
    i = pl.program_id(0)
    j = pl.program_id(1)
    tm = xg_ref.shape[0]
    n_used = used_ref[0]
    active = i < n_used

    def start_gather(tile):
        for c0 in range(0, tm, rc):
            @pl.when(c0 < rows_ref[tile])
            def _(c0=c0):
                def start(r, c):
                    _row_copy(x_hbm, tok_ref[tile * tm + c0 + r], xg_ref, c0 + r, sem).start()
                    return c

                lax.fori_loop(0, rc, start, 0, unroll=8)

    @pl.when(jnp.logical_and(i == 0, j == 0))
    def _():
        start_gather(0)

    @pl.when(jnp.logical_and(j == 1, i + 1 < n_used))
    def _():
        start_gather(i + 1)

    @pl.when(jnp.logical_and(j == 0, rows_ref[i] < tm))
    def _():
        o_ref[...] = jnp.zeros_like(o_ref)

    @pl.when(jnp.logical_and(active, j == 0))
    def _():
        for c0 in range(0, tm, rc):
            @pl.when(c0 < rows_ref[i])
            def _(c0=c0):
                rows = pl.ds(c0, rc)
                pltpu.make_async_copy(x_hbm.at[rows], xg_ref.at[rows], sem).wait()
        for c0 in range(0, tm, rc):
            @pl.when(c0 < rows_ref[i])
            def _(c0=c0):
                rows = pl.ds(c0, rc)
                xb_ref[rows, :] = xg_ref[rows, :].astype(BF16)

    @pl.when(active)
    def _():
        _ffn_accumulate(xb_ref, wg_ref.at[0], wu_ref.at[0], wd_ref.at[0], wgb_ref, wub_ref, wdb_ref, o_ref,
                        _pick(tm, 512), j == 0, n_rows=rows_ref[i], rc_part=rc)


def _moe(x2, buf_tok, blk_exp, blk_rows, n_used, w_gate, w_up, w_down, tm):
    t, d = x2.shape
    n_exp, _, f = w_gate.shape
    tf = _pick(f, 256)
    rc = _pick(tm, 256)
    n_blocks = blk_exp.shape[0]
    nj = f // tf
    assert nj >= 2

    def w_idx(i, j, exp_ref, rows_ref, used_ref, tok_ref):
        last = used_ref[0] - 1
        return exp_ref[jnp.minimum(i, last)], jnp.where(i <= last, j, nj - 1)

    def wgu_map(i, j, *s):
        e, jj = w_idx(i, j, *s)
        return e, 0, jj

    def wd_map(i, j, *s):
        e, jj = w_idx(i, j, *s)
        return e, jj, 0

    grid_spec = pltpu.PrefetchScalarGridSpec(
        num_scalar_prefetch=4,
        grid=(n_blocks, nj),
        in_specs=[
            pl.BlockSpec(memory_space=pl.ANY),
            pl.BlockSpec((1, d, tf), wgu_map),
            pl.BlockSpec((1, d, tf), wgu_map),
            pl.BlockSpec((1, tf, d), wd_map),
        ],
        out_specs=pl.BlockSpec((tm, d), lambda i, j, *s: (i, 0)),
        scratch_shapes=[pltpu.VMEM((tm, d), F32), pltpu.VMEM((tm, d), BF16), pltpu.VMEM((d, tf), BF16),
                        pltpu.VMEM((d, tf), BF16), pltpu.VMEM((tf, d), BF16), pltpu.SemaphoreType.DMA],
    )
    return pl.pallas_call(
        functools.partial(_moe_kernel, rc=rc),
        out_shape=jax.ShapeDtypeStruct((n_blocks * tm, d), F32),
        grid_spec=grid_spec,
        compiler_params=_params(("arbitrary", "arbitrary")),
        name="moe",
    )(blk_exp, blk_rows, n_used, buf_tok, x2, w_gate, w_up, w_down)


def _combine_ln_kernel(p0_ref, p1_ref, x_ref, gate_ref, y_hbm, g_ref, b_ref, o_ref, ybuf_ref, sems, *, alpha):
    i = pl.program_id(0)
    tm = x_ref.shape[0]
    slot = i % 2

    def start_gather(tile, sl):
        def start(r, c):
            for k, p_ref in enumerate((p0_ref, p1_ref)):
                _row_copy(y_hbm, p_ref[tile * tm + r], ybuf_ref.at[sl, k], r, sems.at[sl, k]).start()
            return c

        lax.fori_loop(0, tm, start, 0, unroll=8)

    @pl.when(i == 0)
    def _():
        start_gather(0, 0)

    @pl.when(i + 1 < pl.num_programs(0))
    def _():
        start_gather(i + 1, 1 - slot)

    for k in range(MOE_TOPK):
        pltpu.make_async_copy(y_hbm.at[pl.ds(0, tm)], ybuf_ref.at[slot, k], sems.at[slot, k]).wait()
    rc = _pick(tm, 256)
    for c0 in range(0, tm, rc):
        rows = pl.ds(c0, rc)
        gates = gate_ref[rows, :]
        y = ybuf_ref[slot, 0, rows, :] * gates[:, 0:1] + ybuf_ref[slot, 1, rows, :] * gates[:, 1:2]
        o_ref[rows, :] = _layer_norm(alpha * x_ref[rows, :] + y, g_ref[...], b_ref[...])


def _combine_ln(x2, gates, y_sorted, pos, g, b, alpha):
    t, d = x2.shape
    tm = _pick(t, 256)
    grid_spec = pltpu.PrefetchScalarGridSpec(
        num_scalar_prefetch=2,
        grid=(t // tm,),
        in_specs=[
            pl.BlockSpec((tm, d), lambda i, *s: (i, 0)),
            pl.BlockSpec((tm, gates.shape[1]), lambda i, *s: (i, 0)),
            pl.BlockSpec(memory_space=pl.ANY),
            pl.BlockSpec((1, d), lambda i, *s: (0, 0)),
            pl.BlockSpec((1, d), lambda i, *s: (0, 0)),
        ],
        out_specs=pl.BlockSpec((tm, d), lambda i, *s: (i, 0)),
        scratch_shapes=[pltpu.VMEM((2, MOE_TOPK, tm, d), F32), pltpu.SemaphoreType.DMA((2, MOE_TOPK))],
    )
    kern = functools.partial(_combine_ln_kernel, alpha=alpha)
    return pl.pallas_call(
        kern,
        out_shape=jax.ShapeDtypeStruct((t, d), F32),
        grid_spec=grid_spec,
        compiler_params=_params(("arbitrary",)),
        name="combine_ln",
    )(pos[:, 0], pos[:, 1], x2, gates, y_sorted, g.reshape(1, d), b.reshape(1, d))


def _rope_tables(s):
    inv = 1.0 / (ROPE_THETA ** (jnp.arange(0, HEAD_DIM, 2, dtype=F32) / HEAD_DIM))
    ang = jnp.arange(s, dtype=F32)[:, None] * inv[None, :]
    ang = jnp.concatenate([ang, ang], axis=-1)
    sign = jnp.where(jnp.arange(HEAD_DIM) < HEAD_DIM // 2, -1.0, 1.0).astype(F32)
    return jnp.cos(ang), jnp.sin(ang) * sign[None, :]


def kernel(x, pool_w, pool_scale, w_kv, moba_wq, moba_wo, ffn_w_gate, ffn_w_up, ffn_w_down,
           moe_router, moe_w_gate, moe_w_up, moe_w_down, ln_mix_g, ln_mix_b, ln_ffn_g, ln_ffn_b):
    bsz, s, d = x.shape
    t = bsz * s
    depth = ln_mix_g.shape[0]
    n_a = pool_w.shape[0]
    alpha = (2.0 * depth) ** 0.25
    cos, sin_signed = _rope_tables(s)
    nb = s // MOBA_BLOCK
    assert s % MOBA_BLOCK == 0 and d % HEAD_DIM == 0
    kv = None
    x2 = x.reshape(t, d)
    xb = None
    for l in range(depth):
        if l < n_a:
            x3, xb3 = _pool_ln(x2.reshape(bsz, s, d), pool_w[l], pool_scale[l], ln_mix_g[l], ln_mix_b[l], alpha)
            x2, xb = x3.reshape(t, d), xb3.reshape(t, d)
            router = None
        else:
            bl = l - n_a
            k, vt, k_mean = kv
            xb = _to_bf16(x2) if xb is None else xb
            q = _proj_rope(xb, _to_bf16(moba_wq[bl]), 0, d, cos, sin_signed, s, with_mean=False,
                           out_scale=MOBA_Q_SCALE)[0]
            att = _moba(q.reshape(bsz, s, d), k.reshape(bsz, s, d), vt, k_mean)
            router = moe_router[l // 2] if l % 2 == 1 else None
            res = _oproj_ln(att.reshape(t, d), _to_bf16(moba_wo[bl]), x2, ln_mix_g[l], ln_mix_b[l], alpha, router)
            x2, xb = res[0], None
        jf = l // 2
        if l % 2 == 0:
            xb = _to_bf16(x2) if xb is None else xb
            x2, xb = _swiglu_ln(x2, xb, ffn_w_gate[jf], ffn_w_up[jf], ffn_w_down[jf], ln_ffn_g[l], ln_ffn_b[l],
                                alpha)
        else:
            n_exp = moe_router.shape[2]
            if router is None:
                raise NotImplementedError("MoE after a pooling mixer is not supported")
            tm = _pick(t * MOE_TOPK, 1024)
            buf_tok, blk_exp, blk_rows, n_used, pos = _route(res[1][:, :MOE_TOPK], tm, n_exp)
            y_sorted = _moe(x2, buf_tok, blk_exp, blk_rows, n_used,
                            moe_w_gate[jf], moe_w_up[jf], moe_w_down[jf], tm)
            x2, xb = _combine_ln(x2, res[2], y_sorted, pos, ln_ffn_g[l], ln_ffn_b[l], alpha), None
        if l == n_a - 1:
            xb = _to_bf16(x2) if xb is None else xb
            wkvb = _to_bf16(w_kv)
            k, k_mean = _proj_rope(xb, wkvb, 0, d, cos, sin_signed, s, with_mean=True)
            vt = _proj_vt(xb, wkvb, d, d, bsz, s)
            kv = (k, vt, k_mean.reshape(bsz, nb, d))
    return x2.reshape(bsz, s, d)
```

```python
import functools

import jax
import jax.numpy as jnp
from jax import lax
from jax.experimental import pallas as pl
from jax.experimental.pallas import tpu as pltpu

HEAD_DIM = 128
MOBA_BLOCK = 256
MOBA_TOPK = 3
ROPE_THETA = 10000.0
POOL_WINDOWS = (2, 4, 8, 16)
MOE_TOPK = 2
LN_EPS = 1e-5
NEG = -1e30
LOG2_E = 1.4426950408889634
MOBA_Q_SCALE = HEAD_DIM ** -0.5 * LOG2_E

V7X_LANES = 128
V7X_SUBLANES = 8
V7X_VMEM_LIMIT_BYTES = 56 * 1024 * 1024

F32 = jnp.float32
BF16 = jnp.bfloat16


def _dot(a, b):
    return jnp.dot(a, b, preferred_element_type=F32)


def _dot_nt(a, b):
    return lax.dot_general(a, b, (((1,), (1,)), ((), ())), preferred_element_type=F32)


def _layer_norm(y, g, b):
    mu = jnp.mean(y, axis=-1, keepdims=True)
    d = y - mu
    var = jnp.mean(d * d, axis=-1, keepdims=True)
    return d * lax.rsqrt(var + LN_EPS) * g + b


def _params(semantics):
    return pltpu.CompilerParams(dimension_semantics=semantics, vmem_limit_bytes=V7X_VMEM_LIMIT_BYTES)


def _pick(dim, pref):
    t = min(dim, pref)
    while dim % t:
        t //= 2
    return t


def _ln_rows(x_ref, o_ref, g_ref, b_ref, alpha, rc):
    tm = o_ref.shape[0]
    for c0 in range(0, tm, rc):
        rows = pl.ds(c0, min(rc, tm - c0))
        o_ref[rows, :] = _layer_norm(alpha * x_ref[rows, :] + o_ref[rows, :], g_ref[...], b_ref[...])


def _pool_ln_kernel(xh_ref, x_ref, w_ref, sc_ref, g_ref, b_ref, o_ref, ob_ref, *, alpha, windows, halo):
    i = pl.program_id(1)
    ts, d = x_ref.shape[1], x_ref.shape[2]
    c = d // len(windows)
    t_pos = i * ts + lax.broadcasted_iota(jnp.int32, (ts, 1), 0)
    for g, w in enumerate(windows):
        cols = pl.ds(g * c, c)
        x = x_ref[0, :, cols]
        prev = jnp.where(i > 0, xh_ref[0, :, cols], 0.0)
        cur = jnp.concatenate([prev, x], axis=0)
        width = 1
        while width < w:
            cur = cur + pltpu.roll(cur, width, axis=0)
            width *= 2
        cnt = jnp.minimum(t_pos + 1, w).astype(F32)
        diff = cur[halo:, :] / cnt - x
        y = _dot(diff.astype(BF16), w_ref[g].astype(BF16))
        o_ref[0, :, cols] = y * sc_ref[:, cols]
    _ln_rows(x_ref.at[0], o_ref.at[0], g_ref, b_ref, alpha, 256)
    ob_ref[0] = o_ref[0].astype(BF16)


def _pool_ln(x, w_groups, scale, g, b, alpha):
    bsz, s, d = x.shape
    windows = POOL_WINDOWS
    assert len(windows) == w_groups.shape[0] and d % len(windows) == 0
    assert all(w & (w - 1) == 0 for w in windows) and list(windows) == sorted(windows)
    halo = -(-max(windows) // V7X_SUBLANES) * V7X_SUBLANES
    ts = _pick(s, 512)
    assert ts % halo == 0
    r = ts // halo
    kern = functools.partial(_pool_ln_kernel, alpha=alpha, windows=windows, halo=halo)
    return pl.pallas_call(
        kern,
        out_shape=[jax.ShapeDtypeStruct(x.shape, F32), jax.ShapeDtypeStruct(x.shape, BF16)],
        grid=(bsz, s // ts),
        in_specs=[
            pl.BlockSpec((1, halo, d), lambda bi, i: (bi, jnp.maximum(i * r - 1, 0), 0)),
            pl.BlockSpec((1, ts, d), lambda bi, i: (bi, i, 0)),
            pl.BlockSpec(w_groups.shape, lambda bi, i: (0, 0, 0)),
            pl.BlockSpec((1, d), lambda bi, i: (0, 0)),
            pl.BlockSpec((1, d), lambda bi, i: (0, 0)),
            pl.BlockSpec((1, d), lambda bi, i: (0, 0)),
        ],
        out_specs=[pl.BlockSpec((1, ts, d), lambda bi, i: (bi, i, 0))] * 2,
        compiler_params=_params(("parallel", "parallel")),
        name="pool_ln",
    )(x, x, w_groups, scale.reshape(1, d), g.reshape(1, d), b.reshape(1, d))


def _ffn_accumulate(xb_ref, wg_ref, wu_ref, wd_ref, wgb_ref, wub_ref, wdb_ref, o_ref, rc, n_rows=None, rc_part=None):
    tm = xb_ref.shape[0]

    def round_weights():
        wgb_ref[...] = wg_ref[...].astype(BF16)
        wub_ref[...] = wu_ref[...].astype(BF16)
        wdb_ref[...] = wd_ref[...].astype(BF16)

    def chunk(c0, size):
        rows = pl.ds(c0, size)
        xb = xb_ref[rows, :]
        a = _dot(xb, wgb_ref[...])
        u = _dot(xb, wub_ref[...])
        h = (a * jax.nn.sigmoid(a)) * u
        o_ref[rows, :] += _dot(h.astype(BF16), wdb_ref[...])

    def full():
        round_weights()
        for c0 in range(0, tm, rc):
            chunk(c0, rc)

    if n_rows is None:
        full()
        return
    pl.when(n_rows == tm)(full)
    n_chunks = (n_rows + rc_part - 1) // rc_part
    for live in range(1, tm // rc_part + 1):
        @pl.when(jnp.logical_and(n_rows < tm, n_chunks == live))
        def _(live=live):
            round_weights()
            for c in range(live):
                chunk(c * rc_part, rc_part)


def _swiglu_ln_kernel(xb_ref, x_hbm, wg_ref, wu_ref, wd_ref, g_ref, b_ref, o_ref, ob_ref, wgb_ref, wub_ref, wdb_ref,
                      xr_ref, sems, *, alpha):
    i = pl.program_id(0)
    j = pl.program_id(1)
    tm = o_ref.shape[0]
    rl = xr_ref.shape[1]

    def x_copy(c):
        return pltpu.make_async_copy(x_hbm.at[pl.ds(i * tm + c * rl, rl)], xr_ref.at[c % 2], sems.at[c % 2])

    @pl.when(j == 0)
    def _():
        x_copy(0).start()
        o_ref[...] = jnp.zeros_like(o_ref)

    _ffn_accumulate(xb_ref, wg_ref, wu_ref, wd_ref, wgb_ref, wub_ref, wdb_ref, o_ref, _pick(tm, 512))

    @pl.when(j == pl.num_programs(1) - 1)
    def _():
        n_chunks = tm // rl
        for c in range(n_chunks):
            if c + 1 < n_chunks:
                x_copy(c + 1).start()
            x_copy(c).wait()
            rows = pl.ds(c * rl, rl)
            y = _layer_norm(alpha * xr_ref[c % 2] + o_ref[rows, :], g_ref[...], b_ref[...])
            o_ref[rows, :] = y
            ob_ref[rows, :] = y.astype(BF16)


def _swiglu_ln(x2, xb, w_gate, w_up, w_down, g, b, alpha):
    t, d = x2.shape
    f = w_gate.shape[1]
    tm = _pick(t, 1024)
    tf = _pick(f, 256)
    rl = _pick(tm, 128)
    kern = functools.partial(_swiglu_ln_kernel, alpha=alpha)
    return pl.pallas_call(
        kern,
        out_shape=[jax.ShapeDtypeStruct((t, d), F32), jax.ShapeDtypeStruct((t, d), BF16)],
        grid=(t // tm, f // tf),
        in_specs=[
            pl.BlockSpec((tm, d), lambda i, j: (i, 0)),
            pl.BlockSpec(memory_space=pl.ANY),
            pl.BlockSpec((d, tf), lambda i, j: (0, j)),
            pl.BlockSpec((d, tf), lambda i, j: (0, j)),
            pl.BlockSpec((tf, d), lambda i, j: (j, 0)),
            pl.BlockSpec((1, d), lambda i, j: (0, 0)),
            pl.BlockSpec((1, d), lambda i, j: (0, 0)),
        ],
        out_specs=[pl.BlockSpec((tm, d), lambda i, j: (i, 0)), pl.BlockSpec((tm, d), lambda i, j: (i, 0))],
        scratch_shapes=[pltpu.VMEM((d, tf), BF16), pltpu.VMEM((d, tf), BF16), pltpu.VMEM((tf, d), BF16),
                        pltpu.VMEM((2, rl, d), F32), pltpu.SemaphoreType.DMA((2,))],
        compiler_params=_params(("parallel", "arbitrary")),
        name="swiglu_ln",
    )(xb, x2, w_gate, w_up, w_down, g.reshape(1, d), b.reshape(1, d))


def _to_bf16_kernel(w_ref, o_ref):
    o_ref[...] = w_ref[...].astype(BF16)


def _to_bf16(w):
    r, c = w.shape
    tr = _pick(r, 512)
    return pl.pallas_call(
        _to_bf16_kernel,
        out_shape=jax.ShapeDtypeStruct((r, c), BF16),
        grid=(r // tr,),
        in_specs=[pl.BlockSpec((tr, c), lambda i: (i, 0))],
        out_specs=pl.BlockSpec((tr, c), lambda i: (i, 0)),
        compiler_params=_params(("parallel",)),
        name="to_bf16",
    )(w)


def _proj_rope_kernel(x_ref, w_ref, cos_ref, sin_ref, o_ref, *km_ref, blk, out_scale):
    tm, tn = o_ref.shape
    for c0 in range(0, tm, blk):
        rows = pl.ds(c0, blk)
        y = _dot(x_ref[rows, :], w_ref[...])
        cos = cos_ref[rows, :]
        sin = sin_ref[rows, :]
        if out_scale != 1.0:
            cos, sin = cos * out_scale, sin * out_scale
        heads = []
        for h in range(tn // HEAD_DIM):
            th = y[:, h * HEAD_DIM:(h + 1) * HEAD_DIM]
            heads.append(th * cos + pltpu.roll(th, HEAD_DIM // 2, axis=1) * sin)
        yr = heads[0] if len(heads) == 1 else jnp.concatenate(heads, axis=1)
        o_ref[rows, :] = yr.astype(BF16)
        if km_ref:
            km_ref[0][0, pl.ds(c0 // blk, 1), :] = jnp.sum(yr, axis=0, keepdims=True) * (1.0 / blk)


def _proj_rope(xb, wb, col0, n_out, cos, sin_signed, s, with_mean, out_scale=1.0):
    t, d = xb.shape
    tm = _pick(s, 1024)
    tn = _pick(n_out, 1024)
    assert tm % MOBA_BLOCK == 0 and tn % HEAD_DIM == 0 and col0 % tn == 0
    s_tiles = s // tm
    jb = col0 // tn
    out_shape = [jax.ShapeDtypeStruct((t, n_out), BF16)]
    out_specs = [pl.BlockSpec((tm, tn), lambda i, j: (i, j))]
    if with_mean:
        out_shape.append(jax.ShapeDtypeStruct((t // tm, tm // MOBA_BLOCK, n_out), F32))
        out_specs.append(pl.BlockSpec((1, tm // MOBA_BLOCK, tn), lambda i, j: (i, 0, j)))
    assert not (with_mean and out_scale != 1.0)
    kern = functools.partial(_proj_rope_kernel, blk=MOBA_BLOCK, out_scale=out_scale)
    return pl.pallas_call(
        kern,
        out_shape=out_shape,
        grid=(t // tm, n_out // tn),
        in_specs=[
            pl.BlockSpec((tm, d), lambda i, j: (i, 0)),
            pl.BlockSpec((d, tn), lambda i, j: (0, j + jb)),
            pl.BlockSpec((tm, HEAD_DIM), lambda i, j: (i % s_tiles, 0)),
            pl.BlockSpec((tm, HEAD_DIM), lambda i, j: (i % s_tiles, 0)),
        ],
        out_specs=out_specs,
        compiler_params=_params(("parallel", "arbitrary")),
        name="proj_rope_k" if with_mean else "proj_rope_q",
    )(xb, wb, cos, sin_signed)


def _proj_vt_kernel(x_ref, w_ref, o_ref, *, blk):
    tm = x_ref.shape[0]
    tn = w_ref.shape[1]
    for c in range(tm // blk):
        yt = _dot(x_ref[pl.ds(c * blk, blk), :], w_ref[...]).T
        for h in range(tn // HEAD_DIM):
            o_ref[0, h, c] = yt[h * HEAD_DIM:(h + 1) * HEAD_DIM, :].astype(BF16)


def _proj_vt(xb, wb, col0, n_out, bsz, s):
    t, d = xb.shape
    tm = _pick(s, 1024)
    tn = _pick(n_out, 1024)
    assert tm % MOBA_BLOCK == 0 and tn % HEAD_DIM == 0 and col0 % tn == 0
    s_tiles = s // tm
    jb = col0 // tn
    hpt = tn // HEAD_DIM
    nbt = tm // MOBA_BLOCK
    kern = functools.partial(_proj_vt_kernel, blk=MOBA_BLOCK)
    return pl.pallas_call(
        kern,
        out_shape=jax.ShapeDtypeStruct((bsz, n_out // HEAD_DIM, s // MOBA_BLOCK, HEAD_DIM, MOBA_BLOCK), BF16),
        grid=(t // tm, n_out // tn),
        in_specs=[
            pl.BlockSpec((tm, d), lambda i, j: (i, 0)),
            pl.BlockSpec((d, tn), lambda i, j: (0, j + jb)),
        ],
        out_specs=pl.BlockSpec((1, hpt, nbt, HEAD_DIM, MOBA_BLOCK),
                               lambda i, j: (i // s_tiles, j, i % s_tiles, 0, 0)),
        compiler_params=_params(("parallel", "arbitrary")),
        name="proj_vt",
    )(xb, wb)


def _moba_kernel(q_ref, k_ref, vt_ref, km_ref, o_ref, bias_ref, sa_ref, sb_ref, ma_ref, mb_ref, m8_ref, l8_ref,
                 acc_ref, **kw):
    even = pl.program_id(2) % 2 == 0
    args = (q_ref, k_ref, vt_ref, km_ref, o_ref, bias_ref, m8_ref, l8_ref, acc_ref)
    pl.when(even)(functools.partial(_moba_step, *args, sa_ref, ma_ref, sb_ref, mb_ref, **kw))
    pl.when(jnp.logical_not(even))(functools.partial(_moba_step, *args, sb_ref, mb_ref, sa_ref, ma_ref, **kw))


def _moba_step(q_ref, k_ref, vt_ref, km_ref, o_ref, bias_ref, m8_ref, l8_ref, acc_ref, s_cur, m_cur, s_prev, m_prev,
               *, topk, heads, unroll):
    i = pl.program_id(2)
    blk = q_ref.shape[1]
    nb = km_ref.shape[1]
    trips = (i + unroll - 1) // unroll
    ones = jnp.ones((V7X_SUBLANES, unroll * blk), BF16)

    def fold(x, op):
        return op(x.reshape(blk // V7X_SUBLANES, V7X_SUBLANES, blk), axis=0)

    def score_blocks(it):
        for un in range(unroll):
            n = it * unroll + un
            nk = jnp.minimum(n, nb - 1)
            rows = pl.ds(pl.multiple_of(nk * blk, blk), blk)
            for h in range(heads):
                hs = pl.ds(h * HEAD_DIM, HEAD_DIM)
                s_n = _dot_nt(k_ref[0, rows, hs], q_ref[0, :, hs]) + bias_ref[h, pl.ds(nk, 1), :]
                s_cur[h, n] = s_n
                m8_ref[h] = jnp.maximum(m8_ref[h], fold(s_n, jnp.max))

    def value_blocks(it):
        blocks = [it * unroll + un for un in range(unroll)]
        for h in range(heads):
            p = jnp.concatenate([jnp.exp2(s_prev[h, n] - m_prev[h]).astype(BF16) for n in blocks], axis=0)
            v = jnp.concatenate([vt_ref[0, h, jnp.minimum(n, nb - 1)] for n in blocks], axis=1)
            l8_ref[h] += _dot(ones, p)
            acc_ref[h] += _dot(v, p)

    @pl.when(i < nb)
    def _():
        blk_id = lax.broadcasted_iota(jnp.int32, (nb, blk), 0)
        past = blk_id < i
        for h in range(heads):
            hs = pl.ds(h * HEAD_DIM, HEAD_DIM)
            gate = _dot_nt(km_ref[0, :, hs].astype(BF16), q_ref[0, :, hs])
            gate = jnp.where(past, gate, NEG)
            rank = jnp.zeros((nb, blk), jnp.int32)
            for m in range(nb):
                gm = gate[m:m + 1, :]
                better = (gm > gate) | ((gm == gate) & (blk_id > m))
                rank = rank + better.astype(jnp.int32)
            bias_ref[h] = jnp.where(past & (rank < topk), 0.0, NEG)
            m8_ref[h] = jnp.full(m8_ref.shape[1:], NEG, F32)

    for h in range(heads):
        l8_ref[h] = jnp.zeros(l8_ref.shape[1:], F32)
        acc_ref[h] = jnp.zeros(acc_ref.shape[1:], F32)

    @pl.when(i < nb)
    def _():
        def both(it, carry):
            score_blocks(it)
            value_blocks(it)
            return carry

        lax.fori_loop(0, trips, both, 0)

    @pl.when(i == nb)
    def _():
        def only_values(it, carry):
            value_blocks(it)
            return carry

        lax.fori_loop(0, trips, only_values, 0)

    @pl.when(i < nb)
    def _():
        kpos = lax.broadcasted_iota(jnp.int32, (blk, blk), 0)
        qpos = lax.broadcasted_iota(jnp.int32, (blk, blk), 1)
        own_rows = pl.ds(pl.multiple_of(i * blk, blk), blk)
        for h in range(heads):
            hs = pl.ds(h * HEAD_DIM, HEAD_DIM)
            s = _dot_nt(k_ref[0, own_rows, hs], q_ref[0, :, hs])
            s = jnp.where(kpos <= qpos, s, NEG)
            s_cur[h, i] = s
            for un in range(1, unroll):
                s_cur[h, i + un] = jnp.full((blk, blk), NEG, F32)
            m_cur[h] = jnp.max(jnp.maximum(m8_ref[h], fold(s, jnp.max)), axis=0, keepdims=True)

    @pl.when(i > 0)
    def _():
        for h in range(heads):
            o_ref[0, :, pl.ds(h * HEAD_DIM, HEAD_DIM)] = (acc_ref[h] / l8_ref[h, 0:1, :]).T.astype(BF16)


def _moba(q, k, vt, k_mean):
    bsz, s, d = q.shape
    n_heads = d // HEAD_DIM
    nb = s // MOBA_BLOCK
    heads = _pick(n_heads, 4)
    unroll = 2
    gw = heads * HEAD_DIM
    kern = functools.partial(_moba_kernel, topk=min(MOBA_TOPK, nb), heads=heads, unroll=unroll)
    return pl.pallas_call(
        kern,
        out_shape=jax.ShapeDtypeStruct((bsz, s, d), BF16),
        grid=(bsz, n_heads // heads, nb + 1),
        in_specs=[
            pl.BlockSpec((1, MOBA_BLOCK, gw), lambda b, h, i: (b, jnp.minimum(i, nb - 1), h)),
            pl.BlockSpec((1, s, gw), lambda b, h, i: (b, 0, h)),
            pl.BlockSpec((1, heads, nb, HEAD_DIM, MOBA_BLOCK), lambda b, h, i: (b, h, 0, 0, 0)),
            pl.BlockSpec((1, nb, gw), lambda b, h, i: (b, 0, h)),
        ],
        out_specs=pl.BlockSpec((1, MOBA_BLOCK, gw), lambda b, h, i: (b, jnp.maximum(i - 1, 0), h)),
        scratch_shapes=[pltpu.VMEM((heads, nb, MOBA_BLOCK), F32),
                        pltpu.VMEM((heads, nb + unroll - 1, MOBA_BLOCK, MOBA_BLOCK), F32),
                        pltpu.VMEM((heads, nb + unroll - 1, MOBA_BLOCK, MOBA_BLOCK), F32),
                        pltpu.VMEM((heads, 1, MOBA_BLOCK), F32),
                        pltpu.VMEM((heads, 1, MOBA_BLOCK), F32),
                        pltpu.VMEM((heads, V7X_SUBLANES, MOBA_BLOCK), F32),
                        pltpu.VMEM((heads, V7X_SUBLANES, MOBA_BLOCK), F32),
                        pltpu.VMEM((heads, HEAD_DIM, MOBA_BLOCK), F32)],
        compiler_params=_params(("parallel", "parallel", "arbitrary")),
        name="moba",
    )(q, k, vt, k_mean)


def _oproj_ln_kernel(a_ref, w_ref, x_hbm, g_ref, b_ref, *refs, alpha, n_exp):
    if n_exp:
        wr_ref, o_ref, idx_ref, gate_ref, x_ref, sem = refs
    else:
        o_ref, x_ref, sem = refs
    kk = pl.program_id(1)
    tm = o_ref.shape[0]
    rc = min(tm, 256)
    x_copy = pltpu.make_async_copy(x_hbm.at[pl.ds(pl.program_id(0) * tm, tm)], x_ref, sem)

    @pl.when(kk == 0)
    def _():
        x_copy.start()
        o_ref[...] = jnp.zeros_like(o_ref)

    for c0 in range(0, tm, rc):
        rows = pl.ds(c0, rc)
        o_ref[rows, :] += _dot(a_ref[rows, :], w_ref[...])

    @pl.when(kk == pl.num_programs(1) - 1)
    def _():
        x_copy.wait()
        for c0 in range(0, tm, rc):
            rows = pl.ds(c0, rc)
            y = _layer_norm(alpha * x_ref[rows, :] + o_ref[rows, :], g_ref[...], b_ref[...])
            o_ref[rows, :] = y
            if n_exp:
                logits = _dot(y.astype(BF16), wr_ref[...])
                lane = lax.broadcasted_iota(jnp.int32, logits.shape, 1)
                big = logits.shape[1]
                logits = jnp.where(lane < n_exp, logits, -jnp.inf)
                m1 = jnp.max(logits, axis=1, keepdims=True)
                i1 = jnp.min(jnp.where(logits == m1, lane, big), axis=1, keepdims=True)
                rest = jnp.where(lane == i1, -jnp.inf, logits)
                m2 = jnp.max(rest, axis=1, keepdims=True)
                i2 = jnp.min(jnp.where(rest == m2, lane, big), axis=1, keepdims=True)
                e2 = jnp.exp(m2 - m1)
                den = 1.0 + e2
                idx_ref[rows, :] = jnp.where(lane == 0, i1, jnp.where(lane == 1, i2, 0))
                gate_ref[rows, :] = jnp.where(lane == 0, 1.0 / den, jnp.where(lane == 1, e2 / den, 0.0))


def _oproj_ln(a2, w_o, x2, g, b, alpha, w_router=None):
    t, d = x2.shape
    kdim = a2.shape[1]
    tm = _pick(t, 1024)
    tk = _pick(kdim, 1024)
    n_exp = 0 if w_router is None else w_router.shape[1]
    in_specs = [
        pl.BlockSpec((tm, tk), lambda i, k: (i, k)),
        pl.BlockSpec((tk, d), lambda i, k: (k, 0)),
        pl.BlockSpec(memory_space=pl.ANY),
        pl.BlockSpec((1, d), lambda i, k: (0, 0)),
        pl.BlockSpec((1, d), lambda i, k: (0, 0)),
    ]
    args = [a2, w_o, x2, g.reshape(1, d), b.reshape(1, d)]
    out_shape = [jax.ShapeDtypeStruct((t, d), F32)]
    out_specs = [pl.BlockSpec((tm, d), lambda i, k: (i, 0))]
    if n_exp:
        assert MOE_TOPK == 2 and n_exp <= V7X_LANES
        wr = jnp.pad(w_router, ((0, 0), (0, V7X_LANES - n_exp))).astype(BF16)
        in_specs.append(pl.BlockSpec((d, V7X_LANES), lambda i, k: (0, 0)))
        args.append(wr)
        out_shape += [jax.ShapeDtypeStruct((t, V7X_LANES), jnp.int32), jax.ShapeDtypeStruct((t, V7X_LANES), F32)]
        out_specs += [pl.BlockSpec((tm, V7X_LANES), lambda i, k: (i, 0))] * 2
    kern = functools.partial(_oproj_ln_kernel, alpha=alpha, n_exp=n_exp)
    return pl.pallas_call(
        kern,
        out_shape=out_shape,
        grid=(t // tm, kdim // tk),
        in_specs=in_specs,
        out_specs=out_specs,
        scratch_shapes=[pltpu.VMEM((tm, d), F32), pltpu.SemaphoreType.DMA],
        compiler_params=_params(("parallel", "arbitrary")),
        name="oproj_ln",
    )(*args)


def _route(top_e, tm, n_exp):
    t = top_e.shape[0]
    n_assign = t * MOE_TOPK
    a_exp = top_e.reshape(-1)
    onehot = (a_exp[:, None] == jnp.arange(n_exp, dtype=jnp.int32)[None, :]).astype(jnp.int32)
    csum = jnp.cumsum(onehot, axis=0)
    rank = jnp.sum(csum * onehot, axis=1) - 1
    counts = csum[-1]
    p_counts = (counts + tm - 1) // tm * tm
    p_ends = jnp.cumsum(p_counts)
    p_starts = p_ends - p_counts
    dest = (p_starts[a_exp] + rank).astype(jnp.int32)
    n_blocks = -(-n_assign // tm) + n_exp
    p_rows = n_blocks * tm
    a_tok = jnp.arange(n_assign, dtype=jnp.int32) // MOE_TOPK
    buf_tok = jnp.zeros((p_rows,), jnp.int32).at[dest].set(a_tok)
    blk_start = jnp.arange(n_blocks, dtype=jnp.int32) * tm
    blk_exp = jnp.minimum(jnp.searchsorted(p_ends, blk_start, side="right"), n_exp - 1).astype(jnp.int32)
    blk_rows = jnp.clip(p_starts[blk_exp] + counts[blk_exp] - blk_start, 0, tm).astype(jnp.int32)
    n_used = (p_ends[-1] // tm).astype(jnp.int32).reshape(1)
    return buf_tok, blk_exp, blk_rows, n_used, dest.reshape(t, MOE_TOPK)


def _row_copy(src_hbm, row, dst_vmem, r, sem):
    return pltpu.make_async_copy(src_hbm.at[pl.ds(row, 1)], dst_vmem.at[pl.ds(r, 1)], sem)


def _moe_kernel(exp_ref, rows_ref, used_ref, tok_ref, x_hbm, wg_ref, wu_ref, wd_ref, o_ref,
                xg_ref, xb_ref, wgb_ref, wub_ref, wdb_ref, sem, *, rc):
    i = pl.program_id(0)
    j = pl.program_id(1)
    tm = xg_ref.shape[0]
    n_used = used_ref[0]
    active = i < n_used

    def start_gather(tile):
        for c0 in range(0, tm, rc):
            @pl.when(c0 < rows_ref[tile])
            def _(c0=c0):
                def start(r, c):
                    _row_copy(x_hbm, tok_ref[tile * tm + c0 + r], xg_ref, c0 + r, sem).start()
                    return c

                lax.fori_loop(0, rc, start, 0, unroll=8)

    @pl.when(jnp.logical_and(i == 0, j == 0))
    def _():
        start_gather(0)

    @pl.when(jnp.logical_and(j == 1, i + 1 < n_used))
    def _():
        start_gather(i + 1)

    @pl.when(j == 0)
    def _():
        o_ref[...] = jnp.zeros_like(o_ref)

    @pl.when(jnp.logical_and(active, j == 0))
    def _():
        for c0 in range(0, tm, rc):
            @pl.when(c0 < rows_ref[i])
            def _(c0=c0):
                rows = pl.ds(c0, rc)
                pltpu.make_async_copy(x_hbm.at[rows], xg_ref.at[rows], sem).wait()
        for c0 in range(0, tm, rc):
            @pl.when(c0 < rows_ref[i])
            def _(c0=c0):
                rows = pl.ds(c0, rc)
                xb_ref[rows, :] = xg_ref[rows, :].astype(BF16)

    @pl.when(active)
    def _():
        _ffn_accumulate(xb_ref, wg_ref.at[0], wu_ref.at[0], wd_ref.at[0], wgb_ref, wub_ref, wdb_ref, o_ref,
                        _pick(tm, 512), n_rows=rows_ref[i], rc_part=rc)


def _moe(x2, buf_tok, blk_exp, blk_rows, n_used, w_gate, w_up, w_down, tm):
    t, d = x2.shape
    n_exp, _, f = w_gate.shape
    tf = _pick(f, 256)
    rc = _pick(tm, 256)
    n_blocks = blk_exp.shape[0]
    nj = f // tf
    assert nj >= 2

    def w_idx(i, j, exp_ref, rows_ref, used_ref, tok_ref):
        last = used_ref[0] - 1
        return exp_ref[jnp.minimum(i, last)], jnp.where(i <= last, j, nj - 1)

    def wgu_map(i, j, *s):
        e, jj = w_idx(i, j, *s)
        return e, 0, jj

    def wd_map(i, j, *s):
        e, jj = w_idx(i, j, *s)
        return e, jj, 0

    grid_spec = pltpu.PrefetchScalarGridSpec(
        num_scalar_prefetch=4,
        grid=(n_blocks, nj),
        in_specs=[
            pl.BlockSpec(memory_space=pl.ANY),
            pl.BlockSpec((1, d, tf), wgu_map),
            pl.BlockSpec((1, d, tf), wgu_map),
            pl.BlockSpec((1, tf, d), wd_map),
        ],
        out_specs=pl.BlockSpec((tm, d), lambda i, j, *s: (i, 0)),
        scratch_shapes=[pltpu.VMEM((tm, d), F32), pltpu.VMEM((tm, d), BF16), pltpu.VMEM((d, tf), BF16),
                        pltpu.VMEM((d, tf), BF16), pltpu.VMEM((tf, d), BF16), pltpu.SemaphoreType.DMA],
    )
    return pl.pallas_call(
        functools.partial(_moe_kernel, rc=rc),
        out_shape=jax.ShapeDtypeStruct((n_blocks * tm, d), F32),
        grid_spec=grid_spec,
        compiler_params=_params(("arbitrary", "arbitrary")),
        name="moe",
    )(blk_exp, blk_rows, n_used, buf_tok, x2, w_gate, w_up, w_down)


def _combine_ln_kernel(p0_ref, p1_ref, x_ref, gate_ref, y_hbm, g_ref, b_ref, o_ref, ybuf_ref, sems, *, alpha):
    i = pl.program_id(0)
    tm = x_ref.shape[0]
    slot = i % 2

    def start_gather(tile, sl):
        def start(r, c):
            for k, p_ref in enumerate((p0_ref, p1_ref)):
                _row_copy(y_hbm, p_ref[tile * tm + r], ybuf_ref.at[sl, k], r, sems.at[sl, k]).start()
            return c

        lax.fori_loop(0, tm, start, 0, unroll=8)

    @pl.when(i == 0)
    def _():
        start_gather(0, 0)

    @pl.when(i + 1 < pl.num_programs(0))
    def _():
        start_gather(i + 1, 1 - slot)

    for k in range(MOE_TOPK):
        pltpu.make_async_copy(y_hbm.at[pl.ds(0, tm)], ybuf_ref.at[slot, k], sems.at[slot, k]).wait()
    rc = _pick(tm, 256)
    for c0 in range(0, tm, rc):
        rows = pl.ds(c0, rc)
        gates = gate_ref[rows, :]
        y = ybuf_ref[slot, 0, rows, :] * gates[:, 0:1] + ybuf_ref[slot, 1, rows, :] * gates[:, 1:2]
        o_ref[rows, :] = _layer_norm(alpha * x_ref[rows, :] + y, g_ref[...], b_ref[...])


def _combine_ln(x2, gates, y_sorted, pos, g, b, alpha):
    t, d = x2.shape
    tm = _pick(t, 256)
    grid_spec = pltpu.PrefetchScalarGridSpec(
        num_scalar_prefetch=2,
        grid=(t // tm,),
        in_specs=[
            pl.BlockSpec((tm, d), lambda i, *s: (i, 0)),
            pl.BlockSpec((tm, gates.shape[1]), lambda i, *s: (i, 0)),
            pl.BlockSpec(memory_space=pl.ANY),
            pl.BlockSpec((1, d), lambda i, *s: (0, 0)),
            pl.BlockSpec((1, d), lambda i, *s: (0, 0)),
        ],
        out_specs=pl.BlockSpec((tm, d), lambda i, *s: (i, 0)),
        scratch_shapes=[pltpu.VMEM((2, MOE_TOPK, tm, d), F32), pltpu.SemaphoreType.DMA((2, MOE_TOPK))],
    )
    kern = functools.partial(_combine_ln_kernel, alpha=alpha)
    return pl.pallas_call(
        kern,
        out_shape=jax.ShapeDtypeStruct((t, d), F32),
        grid_spec=grid_spec,
        compiler_params=_params(("arbitrary",)),
        name="combine_ln",
    )(pos[:, 0], pos[:, 1], x2, gates, y_sorted, g.reshape(1, d), b.reshape(1, d))


def _rope_tables(s):
    inv = 1.0 / (ROPE_THETA ** (jnp.arange(0, HEAD_DIM, 2, dtype=F32) / HEAD_DIM))
    ang = jnp.arange(s, dtype=F32)[:, None] * inv[None, :]
    ang = jnp.concatenate([ang, ang], axis=-1)
    sign = jnp.where(jnp.arange(HEAD_DIM) < HEAD_DIM // 2, -1.0, 1.0).astype(F32)
    return jnp.cos(ang), jnp.sin(ang) * sign[None, :]


def kernel(x, pool_w, pool_scale, w_kv, moba_wq, moba_wo, ffn_w_gate, ffn_w_up, ffn_w_down,
           moe_router, moe_w_gate, moe_w_up, moe_w_down, ln_mix_g, ln_mix_b, ln_ffn_g, ln_ffn_b):
    bsz, s, d = x.shape
    t = bsz * s
    depth = ln_mix_g.shape[0]
    n_a = pool_w.shape[0]
    alpha = (2.0 * depth) ** 0.25
    cos, sin_signed = _rope_tables(s)
    nb = s // MOBA_BLOCK
    assert s % MOBA_BLOCK == 0 and d % HEAD_DIM == 0
    kv = None
    x2 = x.reshape(t, d)
    xb = None
    for l in range(depth):
        if l < n_a:
            x3, xb3 = _pool_ln(x2.reshape(bsz, s, d), pool_w[l], pool_scale[l], ln_mix_g[l], ln_mix_b[l], alpha)
            x2, xb = x3.reshape(t, d), xb3.reshape(t, d)
            router = None
        else:
            bl = l - n_a
            k, vt, k_mean = kv
            xb = _to_bf16(x2) if xb is None else xb
            q = _proj_rope(xb, _to_bf16(moba_wq[bl]), 0, d, cos, sin_signed, s, with_mean=False,
                           out_scale=MOBA_Q_SCALE)[0]
            att = _moba(q.reshape(bsz, s, d), k.reshape(bsz, s, d), vt, k_mean)
            router = moe_router[l // 2] if l % 2 == 1 else None
            res = _oproj_ln(att.reshape(t, d), _to_bf16(moba_wo[bl]), x2, ln_mix_g[l], ln_mix_b[l], alpha, router)
            x2, xb = res[0], None
        jf = l // 2
        if l % 2 == 0:
            xb = _to_bf16(x2) if xb is None else xb
            x2, xb = _swiglu_ln(x2, xb, ffn_w_gate[jf], ffn_w_up[jf], ffn_w_down[jf], ln_ffn_g[l], ln_ffn_b[l],
                                alpha)
        else:
            n_exp = moe_router.shape[2]
            if router is None:
                raise NotImplementedError("MoE after a pooling mixer is not supported")
            tm = _pick(t * MOE_TOPK, 1024)
            buf_tok, blk_exp, blk_rows, n_used, pos = _route(res[1][:, :MOE_TOPK], tm, n_exp)
            y_sorted = _moe(x2, buf_tok, blk_exp, blk_rows, n_used,
                            moe_w_gate[jf], moe_w_up[jf], moe_w_down[jf], tm)
            x2, xb = _combine_ln(x2, res[2], y_sorted, pos, ln_ffn_g[l], ln_ffn_b[l], alpha), None
        if l == n_a - 1:
            xb = _to_bf16(x2) if xb is None else xb
            wkvb = _to_bf16(w_kv)
            k, k_mean = _proj_rope(xb, wkvb, 0, d, cos, sin_signed, s, with_mean=True)
            vt = _proj_vt(xb, wkvb, d, d, bsz, s)
            kv = (k, vt, k_mean.reshape(bsz, nb, d))
    return x2.reshape(bsz, s, d)
```

```python
import functools

import jax
import jax.numpy as jnp
from jax import lax
from jax.experimental import pallas as pl
from jax.experimental.pallas import tpu as pltpu

HEAD_DIM = 128
MOBA_BLOCK = 256
MOBA_TOPK = 3
ROPE_THETA = 10000.0
POOL_WINDOWS = (2, 4, 8, 16)
MOE_TOPK = 2
LN_EPS = 1e-5
NEG = -1e30
LOG2_E = 1.4426950408889634
MOBA_Q_SCALE = HEAD_DIM ** -0.5 * LOG2_E

V7X_LANES = 128
V7X_SUBLANES = 8
V7X_VMEM_LIMIT_BYTES = 56 * 1024 * 1024

F32 = jnp.float32
BF16 = jnp.bfloat16


def _dot(a, b):
    return jnp.dot(a, b, preferred_element_type=F32)


def _dot_nt(a, b):
    return lax.dot_general(a, b, (((1,), (1,)), ((), ())), preferred_element_type=F32)


def _layer_norm(y, g, b):
    mu = jnp.mean(y, axis=-1, keepdims=True)
    d = y - mu
    var = jnp.mean(d * d, axis=-1, keepdims=True)
    return d * lax.rsqrt(var + LN_EPS) * g + b


def _params(semantics):
    return pltpu.CompilerParams(dimension_semantics=semantics, vmem_limit_bytes=V7X_VMEM_LIMIT_BYTES)


def _pick(dim, pref):
    t = min(dim, pref)
    while dim % t:
        t //= 2
    return t


def _ln_rows(x_ref, o_ref, g_ref, b_ref, alpha, rc):
    tm = o_ref.shape[0]
    for c0 in range(0, tm, rc):
        rows = pl.ds(c0, min(rc, tm - c0))
        o_ref[rows, :] = _layer_norm(alpha * x_ref[rows, :] + o_ref[rows, :], g_ref[...], b_ref[...])


def _pool_ln_kernel(xh_ref, x_ref, w_ref, sc_ref, g_ref, b_ref, o_ref, ob_ref, *, alpha, windows, halo):
    i = pl.program_id(1)
    ts, d = x_ref.shape[1], x_ref.shape[2]
    c = d // len(windows)
    t_pos = i * ts + lax.broadcasted_iota(jnp.int32, (ts, 1), 0)
    for g, w in enumerate(windows):
        cols = pl.ds(g * c, c)
        x = x_ref[0, :, cols]
        prev = jnp.where(i > 0, xh_ref[0, :, cols], 0.0)
        cur = jnp.concatenate([prev, x], axis=0)
        width = 1
        while width < w:
            cur = cur + pltpu.roll(cur, width, axis=0)
            width *= 2
        cnt = jnp.minimum(t_pos + 1, w).astype(F32)
        diff = cur[halo:, :] / cnt - x
        y = _dot(diff.astype(BF16), w_ref[g].astype(BF16))
        o_ref[0, :, cols] = y * sc_ref[:, cols]
    _ln_rows(x_ref.at[0], o_ref.at[0], g_ref, b_ref, alpha, 256)
    ob_ref[0] = o_ref[0].astype(BF16)


def _pool_ln(x, w_groups, scale, g, b, alpha):
    bsz, s, d = x.shape
    windows = POOL_WINDOWS
    assert len(windows) == w_groups.shape[0] and d % len(windows) == 0
    assert all(w & (w - 1) == 0 for w in windows) and list(windows) == sorted(windows)
    halo = -(-max(windows) // V7X_SUBLANES) * V7X_SUBLANES
    ts = _pick(s, 512)
    assert ts % halo == 0
    r = ts // halo
    kern = functools.partial(_pool_ln_kernel, alpha=alpha, windows=windows, halo=halo)
    return pl.pallas_call(
        kern,
        out_shape=[jax.ShapeDtypeStruct(x.shape, F32), jax.ShapeDtypeStruct(x.shape, BF16)],
        grid=(bsz, s // ts),
        in_specs=[
            pl.BlockSpec((1, halo, d), lambda bi, i: (bi, jnp.maximum(i * r - 1, 0), 0)),
            pl.BlockSpec((1, ts, d), lambda bi, i: (bi, i, 0)),
            pl.BlockSpec(w_groups.shape, lambda bi, i: (0, 0, 0)),
            pl.BlockSpec((1, d), lambda bi, i: (0, 0)),
            pl.BlockSpec((1, d), lambda bi, i: (0, 0)),
            pl.BlockSpec((1, d), lambda bi, i: (0, 0)),
        ],
        out_specs=[pl.BlockSpec((1, ts, d), lambda bi, i: (bi, i, 0))] * 2,
        compiler_params=_params(("parallel", "parallel")),
        name="pool_ln",
    )(x, x, w_groups, scale.reshape(1, d), g.reshape(1, d), b.reshape(1, d))


def _ffn_accumulate(xb_ref, wg_ref, wu_ref, wd_ref, wgb_ref, wub_ref, wdb_ref, o_ref, rc, n_rows=None, rc_part=None):
    tm = xb_ref.shape[0]

    def round_weights():
        wgb_ref[...] = wg_ref[...].astype(BF16)
        wub_ref[...] = wu_ref[...].astype(BF16)
        wdb_ref[...] = wd_ref[...].astype(BF16)

    def chunk(c0, size):
        rows = pl.ds(c0, size)
        xb = xb_ref[rows, :]
        a = _dot(xb, wgb_ref[...])
        u = _dot(xb, wub_ref[...])
        h = (a * jax.nn.sigmoid(a)) * u
        o_ref[rows, :] += _dot(h.astype(BF16), wdb_ref[...])

    def full():
        round_weights()
        for c0 in range(0, tm, rc):
            chunk(c0, rc)

    if n_rows is None:
        full()
        return
    pl.when(n_rows == tm)(full)
    n_chunks = (n_rows + rc_part - 1) // rc_part
    for live in range(1, tm // rc_part + 1):
        @pl.when(jnp.logical_and(n_rows < tm, n_chunks == live))
        def _(live=live):
            round_weights()
            for c in range(live):
                chunk(c * rc_part, rc_part)


def _swiglu_ln_kernel(xb_ref, x_hbm, wg_ref, wu_ref, wd_ref, g_ref, b_ref, o_ref, ob_ref, wgb_ref, wub_ref, wdb_ref,
                      xr_ref, sems, *, alpha):
    i = pl.program_id(0)
    j = pl.program_id(1)
    tm = o_ref.shape[0]
    rl = xr_ref.shape[1]

    def x_copy(c):
        return pltpu.make_async_copy(x_hbm.at[pl.ds(i * tm + c * rl, rl)], xr_ref.at[c % 2], sems.at[c % 2])

    @pl.when(j == 0)
    def _():
        x_copy(0).start()
        o_ref[...] = jnp.zeros_like(o_ref)

    _ffn_accumulate(xb_ref, wg_ref, wu_ref, wd_ref, wgb_ref, wub_ref, wdb_ref, o_ref, _pick(tm, 512))

    @pl.when(j == pl.num_programs(1) - 1)
    def _():
        n_chunks = tm // rl
        for c in range(n_chunks):
            if c + 1 < n_chunks:
                x_copy(c + 1).start()
            x_copy(c).wait()
            rows = pl.ds(c * rl, rl)
            y = _layer_norm(alpha * xr_ref[c % 2] + o_ref[rows, :], g_ref[...], b_ref[...])
            o_ref[rows, :] = y
            ob_ref[rows, :] = y.astype(BF16)


def _swiglu_ln(x2, xb, w_gate, w_up, w_down, g, b, alpha):
    t, d = x2.shape
    f = w_gate.shape[1]
    tm = _pick(t, 1024)
    tf = _pick(f, 256)
    rl = _pick(tm, 128)
    kern = functools.partial(_swiglu_ln_kernel, alpha=alpha)
    return pl.pallas_call(
        kern,
        out_shape=[jax.ShapeDtypeStruct((t, d), F32), jax.ShapeDtypeStruct((t, d), BF16)],
        grid=(t // tm, f // tf),
        in_specs=[
            pl.BlockSpec((tm, d), lambda i, j: (i, 0)),
            pl.BlockSpec(memory_space=pl.ANY),
            pl.BlockSpec((d, tf), lambda i, j: (0, j)),
            pl.BlockSpec((d, tf), lambda i, j: (0, j)),
            pl.BlockSpec((tf, d), lambda i, j: (j, 0)),
            pl.BlockSpec((1, d), lambda i, j: (0, 0)),
            pl.BlockSpec((1, d), lambda i, j: (0, 0)),
        ],
        out_specs=[pl.BlockSpec((tm, d), lambda i, j: (i, 0)), pl.BlockSpec((tm, d), lambda i, j: (i, 0))],
        scratch_shapes=[pltpu.VMEM((d, tf), BF16), pltpu.VMEM((d, tf), BF16), pltpu.VMEM((tf, d), BF16),
                        pltpu.VMEM((2, rl, d), F32), pltpu.SemaphoreType.DMA((2,))],
        compiler_params=_params(("parallel", "arbitrary")),
        name="swiglu_ln",
    )(xb, x2, w_gate, w_up, w_down, g.reshape(1, d), b.reshape(1, d))


def _to_bf16_kernel(w_ref, o_ref):
    o_ref[...] = w_ref[...].astype(BF16)


def _to_bf16(w):
    r, c = w.shape
    tr = _pick(r, 512)
    return pl.pallas_call(
        _to_bf16_kernel,
        out_shape=jax.ShapeDtypeStruct((r, c), BF16),
        grid=(r // tr,),
        in_specs=[pl.BlockSpec((tr, c), lambda i: (i, 0))],
        out_specs=pl.BlockSpec((tr, c), lambda i: (i, 0)),
        compiler_params=_params(("parallel",)),
        name="to_bf16",
    )(w)


def _proj_rope_kernel(x_ref, w_ref, cos_ref, sin_ref, o_ref, *km_ref, blk, out_scale):
    tm, tn = o_ref.shape
    for c0 in range(0, tm, blk):
        rows = pl.ds(c0, blk)
        y = _dot(x_ref[rows, :], w_ref[...])
        cos = cos_ref[rows, :]
        sin = sin_ref[rows, :]
        if out_scale != 1.0:
            cos, sin = cos * out_scale, sin * out_scale
        heads = []
        for h in range(tn // HEAD_DIM):
            th = y[:, h * HEAD_DIM:(h + 1) * HEAD_DIM]
            heads.append(th * cos + pltpu.roll(th, HEAD_DIM // 2, axis=1) * sin)
        yr = heads[0] if len(heads) == 1 else jnp.concatenate(heads, axis=1)
        o_ref[rows, :] = yr.astype(BF16)
        if km_ref:
            km_ref[0][0, pl.ds(c0 // blk, 1), :] = jnp.sum(yr, axis=0, keepdims=True) * (1.0 / blk)


def _proj_rope(xb, wb, col0, n_out, cos, sin_signed, s, with_mean, out_scale=1.0):
    t, d = xb.shape
    tm = _pick(s, 1024)
    tn = _pick(n_out, 1024)
    assert tm % MOBA_BLOCK == 0 and tn % HEAD_DIM == 0 and col0 % tn == 0
    s_tiles = s // tm
    jb = col0 // tn
    out_shape = [jax.ShapeDtypeStruct((t, n_out), BF16)]
    out_specs = [pl.BlockSpec((tm, tn), lambda i, j: (i, j))]
    if with_mean:
        out_shape.append(jax.ShapeDtypeStruct((t // tm, tm // MOBA_BLOCK, n_out), F32))
        out_specs.append(pl.BlockSpec((1, tm // MOBA_BLOCK, tn), lambda i, j: (i, 0, j)))
    assert not (with_mean and out_scale != 1.0)
    kern = functools.partial(_proj_rope_kernel, blk=MOBA_BLOCK, out_scale=out_scale)
    return pl.pallas_call(
        kern,
        out_shape=out_shape,
        grid=(t // tm, n_out // tn),
        in_specs=[
            pl.BlockSpec((tm, d), lambda i, j: (i, 0)),
            pl.BlockSpec((d, tn), lambda i, j: (0, j + jb)),
            pl.BlockSpec((tm, HEAD_DIM), lambda i, j: (i % s_tiles, 0)),
            pl.BlockSpec((tm, HEAD_DIM), lambda i, j: (i % s_tiles, 0)),
        ],
        out_specs=out_specs,
        compiler_params=_params(("parallel", "arbitrary")),
        name="proj_rope_k" if with_mean else "proj_rope_q",
    )(xb, wb, cos, sin_signed)


def _proj_vt_kernel(x_ref, w_ref, o_ref, *, blk):
    tm = x_ref.shape[0]
    tn = w_ref.shape[1]
    for c in range(tm // blk):
        yt = _dot(x_ref[pl.ds(c * blk, blk), :], w_ref[...]).T
        for h in range(tn // HEAD_DIM):
            o_ref[0, h, c] = yt[h * HEAD_DIM:(h + 1) * HEAD_DIM, :].astype(BF16)


def _proj_vt(xb, wb, col0, n_out, bsz, s):
    t, d = xb.shape
    tm = _pick(s, 1024)
    tn = _pick(n_out, 1024)
    assert tm % MOBA_BLOCK == 0 and tn % HEAD_DIM == 0 and col0 % tn == 0
    s_tiles = s // tm
    jb = col0 // tn
    hpt = tn // HEAD_DIM
    nbt = tm // MOBA_BLOCK
    kern = functools.partial(_proj_vt_kernel, blk=MOBA_BLOCK)
    return pl.pallas_call(
        kern,
        out_shape=jax.ShapeDtypeStruct((bsz, n_out // HEAD_DIM, s // MOBA_BLOCK, HEAD_DIM, MOBA_BLOCK), BF16),
        grid=(t // tm, n_out // tn),
        in_specs=[
            pl.BlockSpec((tm, d), lambda i, j: (i, 0)),
            pl.BlockSpec((d, tn), lambda i, j: (0, j + jb)),
        ],
        out_specs=pl.BlockSpec((1, hpt, nbt, HEAD_DIM, MOBA_BLOCK),
                               lambda i, j: (i // s_tiles, j, i % s_tiles, 0, 0)),
        compiler_params=_params(("parallel", "arbitrary")),
        name="proj_vt",
    )(xb, wb)


def _moba_kernel(q_ref, k_ref, vt_ref, km_ref, o_ref, bias_ref, sa_ref, sb_ref, ma_ref, mb_ref, m8_ref, l8_ref,
                 acc_ref, **kw):
    even = pl.program_id(2) % 2 == 0
    args = (q_ref, k_ref, vt_ref, km_ref, o_ref, bias_ref, m8_ref, l8_ref, acc_ref)
    pl.when(even)(functools.partial(_moba_step, *args, sa_ref, ma_ref, sb_ref, mb_ref, **kw))
    pl.when(jnp.logical_not(even))(functools.partial(_moba_step, *args, sb_ref, mb_ref, sa_ref, ma_ref, **kw))


def _moba_step(q_ref, k_ref, vt_ref, km_ref, o_ref, bias_ref, m8_ref, l8_ref, acc_ref, s_cur, m_cur, s_prev, m_prev,
               *, topk, heads, unroll):
    i = pl.program_id(2)
    blk = q_ref.shape[1]
    nb = km_ref.shape[1]
    trips = (i + unroll - 1) // unroll
    ones = jnp.ones((V7X_SUBLANES, unroll * blk), BF16)

    def fold(x, op):
        return op(x.reshape(blk // V7X_SUBLANES, V7X_SUBLANES, blk), axis=0)

    def score_blocks(it):
        for un in range(unroll):
            n = it * unroll + un
            nk = jnp.minimum(n, nb - 1)
            rows = pl.ds(pl.multiple_of(nk * blk, blk), blk)
            for h in range(heads):
                hs = pl.ds(h * HEAD_DIM, HEAD_DIM)
                s_n = _dot_nt(k_ref[0, rows, hs], q_ref[0, :, hs]) + bias_ref[h, pl.ds(nk, 1), :]
                s_cur[h, n] = s_n
                m8_ref[h] = jnp.maximum(m8_ref[h], fold(s_n, jnp.max))

    def value_blocks(it):
        blocks = [it * unroll + un for un in range(unroll)]
        for h in range(heads):
            p = jnp.concatenate([jnp.exp2(s_prev[h, n] - m_prev[h]).astype(BF16) for n in blocks], axis=0)
            v = jnp.concatenate([vt_ref[0, h, jnp.minimum(n, nb - 1)] for n in blocks], axis=1)
            l8_ref[h] += _dot(ones, p)
            acc_ref[h] += _dot(v, p)

    @pl.when(i < nb)
    def _():
        blk_id = lax.broadcasted_iota(jnp.int32, (nb, blk), 0)
        past = blk_id < i
        for h in range(heads):
            hs = pl.ds(h * HEAD_DIM, HEAD_DIM)
            gate = _dot_nt(km_ref[0, :, hs].astype(BF16), q_ref[0, :, hs])
            gate = jnp.where(past, gate, NEG)
            rank = jnp.zeros((nb, blk), jnp.int32)
            for m in range(nb):
                gm = gate[m:m + 1, :]
                better = (gm > gate) | ((gm == gate) & (blk_id > m))
                rank = rank + better.astype(jnp.int32)
            bias_ref[h] = jnp.where(past & (rank < topk), 0.0, NEG)
            m8_ref[h] = jnp.full(m8_ref.shape[1:], NEG, F32)

    for h in range(heads):
        l8_ref[h] = jnp.zeros(l8_ref.shape[1:], F32)
        acc_ref[h] = jnp.zeros(acc_ref.shape[1:], F32)

    @pl.when(i < nb)
    def _():
        def both(it, carry):
            score_blocks(it)
            value_blocks(it)
            return carry

        lax.fori_loop(0, trips, both, 0)

    @pl.when(i == nb)
    def _():
        def only_values(it, carry):
            value_blocks(it)
            return carry

        lax.fori_loop(0, trips, only_values, 0)

    def own_block():
        kpos = lax.broadcasted_iota(jnp.int32, (blk, blk), 0)
        qpos = lax.broadcasted_iota(jnp.int32, (blk, blk), 1)
        own_rows = pl.ds(pl.multiple_of(i * blk, blk), blk)
        for h in range(heads):
            hs = pl.ds(h * HEAD_DIM, HEAD_DIM)
            s = _dot_nt(k_ref[0, own_rows, hs], q_ref[0, :, hs])
            s = jnp.where(kpos <= qpos, s, NEG)
            s_cur[h, i] = s
            for un in range(1, unroll):
                s_cur[h, i + un] = jnp.full((blk, blk), NEG, F32)
            m_cur[h] = jnp.max(jnp.maximum(m8_ref[h], fold(s, jnp.max)), axis=0, keepdims=True)

    def finish_previous():
        for h in range(heads):
            o_ref[0, :, pl.ds(h * HEAD_DIM, HEAD_DIM)] = (acc_ref[h] / l8_ref[h, 0:1, :]).T.astype(BF16)

    @pl.when(jnp.logical_and(i > 0, i < nb))
    def _():
        own_block()
        finish_previous()

    pl.when(i == 0)(own_block)
    pl.when(i == nb)(finish_previous)


def _moba(q, k, vt, k_mean):
    bsz, s, d = q.shape
    n_heads = d // HEAD_DIM
    nb = s // MOBA_BLOCK
    heads = _pick(n_heads, 4)
    unroll = 2
    gw = heads * HEAD_DIM
    kern = functools.partial(_moba_kernel, topk=min(MOBA_TOPK, nb), heads=heads, unroll=unroll)
    return pl.pallas_call(
        kern,
        out_shape=jax.ShapeDtypeStruct((bsz, s, d), BF16),
        grid=(bsz, n_heads // heads, nb + 1),
        in_specs=[
            pl.BlockSpec((1, MOBA_BLOCK, gw), lambda b, h, i: (b, jnp.minimum(i, nb - 1), h)),
            pl.BlockSpec((1, s, gw), lambda b, h, i: (b, 0, h)),
            pl.BlockSpec((1, heads, nb, HEAD_DIM, MOBA_BLOCK), lambda b, h, i: (b, h, 0, 0, 0)),
            pl.BlockSpec((1, nb, gw), lambda b, h, i: (b, 0, h)),
        ],
        out_specs=pl.BlockSpec((1, MOBA_BLOCK, gw), lambda b, h, i: (b, jnp.maximum(i - 1, 0), h)),
        scratch_shapes=[pltpu.VMEM((heads, nb, MOBA_BLOCK), F32),
                        pltpu.VMEM((heads, nb + unroll - 1, MOBA_BLOCK, MOBA_BLOCK), F32),
                        pltpu.VMEM((heads, nb + unroll - 1, MOBA_BLOCK, MOBA_BLOCK), F32),
                        pltpu.VMEM((heads, 1, MOBA_BLOCK), F32),
                        pltpu.VMEM((heads, 1, MOBA_BLOCK), F32),
                        pltpu.VMEM((heads, V7X_SUBLANES, MOBA_BLOCK), F32),
                        pltpu.VMEM((heads, V7X_SUBLANES, MOBA_BLOCK), F32),
                        pltpu.VMEM((heads, HEAD_DIM, MOBA_BLOCK), F32)],
        compiler_params=_params(("parallel", "parallel", "arbitrary")),
        name="moba",
    )(q, k, vt, k_mean)


def _oproj_ln_kernel(a_ref, w_ref, x_hbm, g_ref, b_ref, *refs, alpha, n_exp):
    if n_exp:
        wr_ref, o_ref, idx_ref, gate_ref, x_ref, sem = refs
    else:
        o_ref, x_ref, sem = refs
    kk = pl.program_id(1)
    tm = o_ref.shape[0]
    rc = min(tm, 256)
    x_copy = pltpu.make_async_copy(x_hbm.at[pl.ds(pl.program_id(0) * tm, tm)], x_ref, sem)

    @pl.when(kk == 0)
    def _():
        x_copy.start()
        o_ref[...] = jnp.zeros_like(o_ref)

    for c0 in range(0, tm, rc):
        rows = pl.ds(c0, rc)
        o_ref[rows, :] += _dot(a_ref[rows, :], w_ref[...])

    @pl.when(kk == pl.num_programs(1) - 1)
    def _():
        x_copy.wait()
        for c0 in range(0, tm, rc):
            rows = pl.ds(c0, rc)
            y = _layer_norm(alpha * x_ref[rows, :] + o_ref[rows, :], g_ref[...], b_ref[...])
            o_ref[rows, :] = y
            if n_exp:
                logits = _dot(y.astype(BF16), wr_ref[...])
                lane = lax.broadcasted_iota(jnp.int32, logits.shape, 1)
                big = logits.shape[1]
                logits = jnp.where(lane < n_exp, logits, -jnp.inf)
                m1 = jnp.max(logits, axis=1, keepdims=True)
                i1 = jnp.min(jnp.where(logits == m1, lane, big), axis=1, keepdims=True)
                rest = jnp.where(lane == i1, -jnp.inf, logits)
                m2 = jnp.max(rest, axis=1, keepdims=True)
                i2 = jnp.min(jnp.where(rest == m2, lane, big), axis=1, keepdims=True)
                e2 = jnp.exp(m2 - m1)
                den = 1.0 + e2
                idx_ref[rows, :] = jnp.where(lane == 0, i1, jnp.where(lane == 1, i2, 0))
                gate_ref[rows, :] = jnp.where(lane == 0, 1.0 / den, jnp.where(lane == 1, e2 / den, 0.0))


def _oproj_ln(a2, w_o, x2, g, b, alpha, w_router=None):
    t, d = x2.shape
    kdim = a2.shape[1]
    tm = _pick(t, 1024)
    tk = _pick(kdim, 1024)
    n_exp = 0 if w_router is None else w_router.shape[1]
    in_specs = [
        pl.BlockSpec((tm, tk), lambda i, k: (i, k)),
        pl.BlockSpec((tk, d), lambda i, k: (k, 0)),
        pl.BlockSpec(memory_space=pl.ANY),
        pl.BlockSpec((1, d), lambda i, k: (0, 0)),
        pl.BlockSpec((1, d), lambda i, k: (0, 0)),
    ]
    args = [a2, w_o, x2, g.reshape(1, d), b.reshape(1, d)]
    out_shape = [jax.ShapeDtypeStruct((t, d), F32)]
    out_specs = [pl.BlockSpec((tm, d), lambda i, k: (i, 0))]
    if n_exp:
        assert MOE_TOPK == 2 and n_exp <= V7X_LANES
        wr = jnp.pad(w_router, ((0, 0), (0, V7X_LANES - n_exp))).astype(BF16)
        in_specs.append(pl.BlockSpec((d, V7X_LANES), lambda i, k: (0, 0)))
        args.append(wr)
        out_shape += [jax.ShapeDtypeStruct((t, V7X_LANES), jnp.int32), jax.ShapeDtypeStruct((t, V7X_LANES), F32)]
        out_specs += [pl.BlockSpec((tm, V7X_LANES), lambda i, k: (i, 0))] * 2
    kern = functools.partial(_oproj_ln_kernel, alpha=alpha, n_exp=n_exp)
    return pl.pallas_call(
        kern,
        out_shape=out_shape,
        grid=(t // tm, kdim // tk),
        in_specs=in_specs,
        out_specs=out_specs,
        scratch_shapes=[pltpu.VMEM((tm, d), F32), pltpu.SemaphoreType.DMA],
        compiler_params=_params(("parallel", "arbitrary")),
        name="oproj_ln",
    )(*args)


def _route(top_e, tm, n_exp):
    t = top_e.shape[0]
    n_assign = t * MOE_TOPK
    a_exp = top_e.reshape(-1)
    onehot = (a_exp[:, None] == jnp.arange(n_exp, dtype=jnp.int32)[None, :]).astype(jnp.int32)
    csum = jnp.cumsum(onehot, axis=0)
    rank = jnp.sum(csum * onehot, axis=1) - 1
    counts = csum[-1]
    p_counts = (counts + tm - 1) // tm * tm
    p_ends = jnp.cumsum(p_counts)
    p_starts = p_ends - p_counts
    dest = (p_starts[a_exp] + rank).astype(jnp.int32)
    n_blocks = -(-n_assign // tm) + n_exp
    p_rows = n_blocks * tm
    a_tok = jnp.arange(n_assign, dtype=jnp.int32) // MOE_TOPK
    buf_tok = jnp.zeros((p_rows,), jnp.int32).at[dest].set(a_tok)
    blk_start = jnp.arange(n_blocks, dtype=jnp.int32) * tm
    blk_exp = jnp.minimum(jnp.searchsorted(p_ends, blk_start, side="right"), n_exp - 1).astype(jnp.int32)
    blk_rows = jnp.clip(p_starts[blk_exp] + counts[blk_exp] - blk_start, 0, tm).astype(jnp.int32)
    n_used = (p_ends[-1] // tm).astype(jnp.int32).reshape(1)
    return buf_tok, blk_exp, blk_rows, n_used, dest.reshape(t, MOE_TOPK)


def _row_copy(src_hbm, row, dst_vmem, r, sem):
    return pltpu.make_async_copy(src_hbm.at[pl.ds(row, 1)], dst_vmem.at[pl.ds(r, 1)], sem)


def _moe_kernel(exp_ref, rows_ref, used_ref, tok_ref, x_hbm, wg_ref, wu_ref, wd_ref, o_ref,
                xg_ref, xb_ref, wgb_ref, wub_ref, wdb_ref, sem, *, rc):
    i = pl.program_id(0)
    j = pl.program_id(1)
    tm = xg_ref.shape[0]
    n_used = used_ref[0]
    active = i < n_used

    def start_gather(tile):
        for c0 in range(0, tm, rc):
            @pl.when(c0 < rows_ref[tile])
            def _(c0=c0):
                def start(r, c):
                    _row_copy(x_hbm, tok_ref[tile * tm + c0 + r], xg_ref, c0 + r, sem).start()
                    return c

                lax.fori_loop(0, rc, start, 0, unroll=8)

    @pl.when(jnp.logical_and(i == 0, j == 0))
    def _():
        start_gather(0)

    @pl.when(jnp.logical_and(j == 1, i + 1 < n_used))
    def _():
        start_gather(i + 1)

    @pl.when(j == 0)
    def _():
        o_ref[...] = jnp.zeros_like(o_ref)

    @pl.when(jnp.logical_and(active, j == 0))
    def _():
        for c0 in range(0, tm, rc):
            @pl.when(c0 < rows_ref[i])
            def _(c0=c0):
                rows = pl.ds(c0, rc)
                pltpu.make_async_copy(x_hbm.at[rows], xg_ref.at[rows], sem).wait()
        for c0 in range(0, tm, rc):
            @pl.when(c0 < rows_ref[i])
            def _(c0=c0):
                rows = pl.ds(c0, rc)
                xb_ref[rows, :] = xg_ref[rows, :].astype(BF16)

    @pl.when(active)
    def _():
        _ffn_accumulate(xb_ref, wg_ref.at[0], wu_ref.at[0], wd_ref.at[0], wgb_ref, wub_ref, wdb_ref, o_ref,
                        _pick(tm, 512), n_rows=rows_ref[i], rc_part=rc)


def _moe(x2, buf_tok, blk_exp, blk_rows, n_used, w_gate, w_up, w_down, tm):
    t, d = x2.shape
    n_exp, _, f = w_gate.shape
    tf = _pick(f, 256)
    rc = _pick(tm, 256)
    n_blocks = blk_exp.shape[0]
    nj = f // tf
    assert nj >= 2

    def w_idx(i, j, exp_ref, rows_ref, used_ref, tok_ref):
        last = used_ref[0] - 1
        return exp_ref[jnp.minimum(i, last)], jnp.where(i <= last, j, nj - 1)

    def wgu_map(i, j, *s):
        e, jj = w_idx(i, j, *s)
        return e, 0, jj

    def wd_map(i, j, *s):
        e, jj = w_idx(i, j, *s)
        return e, jj, 0

    grid_spec = pltpu.PrefetchScalarGridSpec(
        num_scalar_prefetch=4,
        grid=(n_blocks, nj),
        in_specs=[
            pl.BlockSpec(memory_space=pl.ANY),
            pl.BlockSpec((1, d, tf), wgu_map),
            pl.BlockSpec((1, d, tf), wgu_map),
            pl.BlockSpec((1, tf, d), wd_map),
        ],
        out_specs=pl.BlockSpec((tm, d), lambda i, j, *s: (i, 0)),
        scratch_shapes=[pltpu.VMEM((tm, d), F32), pltpu.VMEM((tm, d), BF16), pltpu.VMEM((d, tf), BF16),
                        pltpu.VMEM((d, tf), BF16), pltpu.VMEM((tf, d), BF16), pltpu.SemaphoreType.DMA],
    )
    return pl.pallas_call(
        functools.partial(_moe_kernel, rc=rc),
        out_shape=jax.ShapeDtypeStruct((n_blocks * tm, d), F32),
        grid_spec=grid_spec,
        compiler_params=_params(("arbitrary", "arbitrary")),
        name="moe",
    )(blk_exp, blk_rows, n_used, buf_tok, x2, w_gate, w_up, w_down)


def _combine_ln_kernel(p0_ref, p1_ref, x_ref, gate_ref, y_hbm, g_ref, b_ref, o_ref, ybuf_ref, sems, *, alpha):
    i = pl.program_id(0)
    tm = x_ref.shape[0]
    slot = i % 2

    def start_gather(tile, sl):
        def start(r, c):
            for k, p_ref in enumerate((p0_ref, p1_ref)):
                _row_copy(y_hbm, p_ref[tile * tm + r], ybuf_ref.at[sl, k], r, sems.at[sl, k]).start(priority=k)
            return c

        lax.fori_loop(0, tm, start, 0, unroll=8)

    @pl.when(i == 0)
    def _():
        start_gather(0, 0)

    @pl.when(i + 1 < pl.num_programs(0))
    def _():
        start_gather(i + 1, 1 - slot)

    for k in range(MOE_TOPK):
        pltpu.make_async_copy(y_hbm.at[pl.ds(0, tm)], ybuf_ref.at[slot, k], sems.at[slot, k]).wait()
    rc = _pick(tm, 256)
    for c0 in range(0, tm, rc):
        rows = pl.ds(c0, rc)
        gates = gate_ref[rows, :]
        y = ybuf_ref[slot, 0, rows, :] * gates[:, 0:1] + ybuf_ref[slot, 1, rows, :] * gates[:, 1:2]
        o_ref[rows, :] = _layer_norm(alpha * x_ref[rows, :] + y, g_ref[...], b_ref[...])


def _combine_ln(x2, gates, y_sorted, pos, g, b, alpha):
    t, d = x2.shape
    tm = _pick(t, 256)
    grid_spec = pltpu.PrefetchScalarGridSpec(
        num_scalar_prefetch=2,
        grid=(t // tm,),
        in_specs=[
            pl.BlockSpec((tm, d), lambda i, *s: (i, 0)),
            pl.BlockSpec((tm, gates.shape[1]), lambda i, *s: (i, 0)),
            pl.BlockSpec(memory_space=pl.ANY),
            pl.BlockSpec((1, d), lambda i, *s: (0, 0)),
            pl.BlockSpec((1, d), lambda i, *s: (0, 0)),
        ],
        out_specs=pl.BlockSpec((tm, d), lambda i, *s: (i, 0)),
        scratch_shapes=[pltpu.VMEM((2, MOE_TOPK, tm, d), F32), pltpu.SemaphoreType.DMA((2, MOE_TOPK))],
    )
    kern = functools.partial(_combine_ln_kernel, alpha=alpha)
    return pl.pallas_call(
        kern,
        out_shape=jax.ShapeDtypeStruct((t, d), F32),
        grid_spec=grid_spec,
        compiler_params=_params(("arbitrary",)),
        name="combine_ln",
    )(pos[:, 0], pos[:, 1], x2, gates, y_sorted, g.reshape(1, d), b.reshape(1, d))


def _rope_tables(s):
    inv = 1.0 / (ROPE_THETA ** (jnp.arange(0, HEAD_DIM, 2, dtype=F32) / HEAD_DIM))
    ang = jnp.arange(s, dtype=F32)[:, None] * inv[None, :]
    ang = jnp.concatenate([ang, ang], axis=-1)
    sign = jnp.where(jnp.arange(HEAD_DIM) < HEAD_DIM // 2, -1.0, 1.0).astype(F32)
    return jnp.cos(ang), jnp.sin(ang) * sign[None, :]


def kernel(x, pool_w, pool_scale, w_kv, moba_wq, moba_wo, ffn_w_gate, ffn_w_up, ffn_w_down,
           moe_router, moe_w_gate, moe_w_up, moe_w_down, ln_mix_g, ln_mix_b, ln_ffn_g, ln_ffn_b):
    bsz, s, d = x.shape
    t = bsz * s
    depth = ln_mix_g.shape[0]
    n_a = pool_w.shape[0]
    alpha = (2.0 * depth) ** 0.25
    cos, sin_signed = _rope_tables(s)
    nb = s // MOBA_BLOCK
    assert s % MOBA_BLOCK == 0 and d % HEAD_DIM == 0
    kv = None
    x2 = x.reshape(t, d)
    xb = None
    for l in range(depth):
        if l < n_a:
            x3, xb3 = _pool_ln(x2.reshape(bsz, s, d), pool_w[l], pool_scale[l], ln_mix_g[l], ln_mix_b[l], alpha)
            x2, xb = x3.reshape(t, d), xb3.reshape(t, d)
            router = None
        else:
            bl = l - n_a
            k, vt, k_mean = kv
            xb = _to_bf16(x2) if xb is None else xb
            q = _proj_rope(xb, _to_bf16(moba_wq[bl]), 0, d, cos, sin_signed, s, with_mean=False,
                           out_scale=MOBA_Q_SCALE)[0]
            att = _moba(q.reshape(bsz, s, d), k.reshape(bsz, s, d), vt, k_mean)
            router = moe_router[l // 2] if l % 2 == 1 else None
            res = _oproj_ln(att.reshape(t, d), _to_bf16(moba_wo[bl]), x2, ln_mix_g[l], ln_mix_b[l], alpha, router)
            x2, xb = res[0], None
        jf = l // 2
        if l % 2 == 0:
            xb = _to_bf16(x2) if xb is None else xb
            x2, xb = _swiglu_ln(x2, xb, ffn_w_gate[jf], ffn_w_up[jf], ffn_w_down[jf], ln_ffn_g[l], ln_ffn_b[l],
                                alpha)
        else:
            n_exp = moe_router.shape[2]
            if router is None:
                raise NotImplementedError("MoE after a pooling mixer is not supported")
            tm = _pick(t * MOE_TOPK, 1024)
            buf_tok, blk_exp, blk_rows, n_used, pos = _route(res[1][:, :MOE_TOPK], tm, n_exp)
            y_sorted = _moe(x2, buf_tok, blk_exp, blk_rows, n_used,
                            moe_w_gate[jf], moe_w_up[jf], moe_w_down[jf], tm)
            x2, xb = _combine_ln(x2, res[2], y_sorted, pos, ln_ffn_g[l], ln_ffn_b[l], alpha), None
        if l == n_a - 1:
            xb = _to_bf16(x2) if xb is None else xb
            wkvb = _to_bf16(w_kv)
            k, k_mean = _proj_rope(xb, wkvb, 0, d, cos, sin_signed, s, with_mean=True)
            vt = _proj_vt(xb, wkvb, d, d, bsz, s)
            kv = (k, vt, k_mean.reshape(bsz, nb, d))
    return x2.reshape(bsz, s, d)
```
